```python
import math
import jax, jax.numpy as jnp
from jax import lax
import numpy as np

D_MODEL = 2048
BATCH = 1
SEQ = 8192
DEPTH = 4

N_A_LAYERS = DEPTH // 2
N_B_LAYERS = DEPTH - N_A_LAYERS
EPS = 1e-6
D_FF = ((8 * D_MODEL // 3 + 255) // 256) * 256
HEAD_DIM_A = 128
N_HEADS_A = D_MODEL // HEAD_DIM_A
DILATED_BRANCHES = ((128, 1), (512, 4), (2048, 16))
N_HEADS_B = D_MODEL // 128
QK_NOPE_DIM = 128
QK_ROPE_DIM = 64
V_HEAD_DIM = 128
KV_LORA_RANK = D_MODEL // 4
Q_LORA_RANK = D_MODEL // 4
ROPE_THETA = 10000.0
Q_BLOCK = 128

kernel_name = "yoco_dilated_swa_mla_macaron"


def rms_norm(x, g):
    xf = x.astype(jnp.float32)
    y = xf * lax.rsqrt(jnp.mean(xf * xf, axis=-1, keepdims=True) + EPS)
    return (y * g.astype(jnp.float32)).astype(x.dtype)


def swiglu(h, w_gate, w_up, w_down):
    return (jax.nn.silu(h @ w_gate) * (h @ w_up)) @ w_down


def alibi_slopes(n_heads):
    return jnp.asarray(2.0 ** (-8.0 * (np.arange(n_heads) + 1) / n_heads), dtype=jnp.float32)


def rope_tables(seq):
    inv = 1.0 / (ROPE_THETA ** (jnp.arange(0, QK_ROPE_DIM, 2, dtype=jnp.float32) / QK_ROPE_DIM))
    ang = jnp.arange(seq, dtype=jnp.float32)[:, None] * inv[None, :]
    return jnp.cos(ang), jnp.sin(ang)


def apply_rope(t, cos, sin):
    tf = t.astype(jnp.float32)
    t1, t2 = jnp.split(tf, 2, axis=-1)
    return jnp.concatenate([t1 * cos - t2 * sin, t1 * sin + t2 * cos], axis=-1).astype(t.dtype)


def dilated_branch(q, k, v, window, dilation, slopes):
    B, S, H, Dh = q.shape
    n = window // dilation
    span = n * dilation
    Sp = -(-S // span) * span
    nb = Sp // span

    def to_blocks(t):
        t = jnp.pad(t, ((0, 0), (0, Sp - S), (0, 0), (0, 0)))
        t = t.reshape(B, Sp // dilation, dilation, H, Dh).transpose(0, 2, 1, 3, 4)
        return t.reshape(B, dilation, nb, n, H, Dh)

    def with_prev(t):
        prev = jnp.pad(t[:, :, :-1], ((0, 0), (0, 0), (1, 0), (0, 0), (0, 0), (0, 0)))
        return jnp.concatenate([prev, t], axis=3)

    qb = to_blocks(q)
    kw = with_prev(to_blocks(k))
    vw = with_prev(to_blocks(v))
    s = jnp.einsum('brcihd,brcjhd->brchij', qb, kw, preferred_element_type=jnp.float32) * (Dh ** -0.5)
    i = jnp.arange(n)[:, None]
    j = jnp.arange(2 * n)[None, :]
    steps = n + i - j
    band = (steps >= 0) & (steps <= n)
    valid = band[None] & ((jnp.arange(nb)[:, None, None] > 0) | (j >= n)[None])
    bias = -slopes[:, None, None] * (dilation * steps).astype(jnp.float32)[None]
    s = jnp.where(valid[None, None, :, None], s + bias[None, None, None], -jnp.inf)
    m = jnp.max(s, axis=-1, keepdims=True)
    p = jnp.exp(s - m)
    l = jnp.sum(p, axis=-1, keepdims=True)
    o = jnp.einsum('brchij,brcjhd->brcihd', (p / l).astype(v.dtype), vw)
    lse = (m + jnp.log(l))[..., 0]
    o = o.reshape(B, dilation, Sp // dilation, H, Dh).transpose(0, 2, 1, 3, 4).reshape(B, Sp, H, Dh)[:, :S]
    lse = lse.transpose(0, 1, 2, 4, 3).reshape(B, dilation, Sp // dilation, H)
    lse = lse.transpose(0, 2, 1, 3).reshape(B, Sp, H)[:, :S]
    return o, lse


def dilated_attention(h, w_qkv, w_o, slopes):
    B, S, _ = h.shape
    qkv = (h @ w_qkv).reshape(B, S, 3, N_HEADS_A, HEAD_DIM_A)
    q, k, v = qkv[:, :, 0], qkv[:, :, 1], qkv[:, :, 2]
    outs, lses = [], []
    for window, dilation in DILATED_BRANCHES:
        o, lse = dilated_branch(q, k, v, window, dilation, slopes)
        outs.append(o)
        lses.append(lse)
    wts = jax.nn.softmax(jnp.stack(lses, axis=0), axis=0)
    o = jnp.einsum('gbsh,gbshd->bshd', wts.astype(q.dtype), jnp.stack(outs, axis=0))
    return o.reshape(B, S, N_HEADS_A * HEAD_DIM_A) @ w_o


def mla_shared_kv(x, kv_norm, b_wdkv, b_ckv_norm, b_wkr, b_wuk, b_wuv, cos, sin):
    h = rms_norm(x, kv_norm)
    c_kv = rms_norm(h @ b_wdkv, b_ckv_norm)
    k_nope = jnp.einsum('bsc,chd->bshd', c_kv, b_wuk)
    v = jnp.einsum('bsc,chd->bshd', c_kv, b_wuv)
    k_rope = apply_rope(h @ b_wkr, cos, sin)
    return k_nope, k_rope, v


def mla_attention(h, k_nope, k_rope, v, w_dq, cq_norm, w_uq, w_o, cos, sin):
    B, S, _ = h.shape
    c_q = rms_norm(h @ w_dq, cq_norm)
    q = jnp.einsum('bsc,chd->bshd', c_q, w_uq)
    q_nope = q[..., :QK_NOPE_DIM]
    q_rope = apply_rope(q[..., QK_NOPE_DIM:], cos[:, None, :], sin[:, None, :])
    nb = S // Q_BLOCK
    qn_b = q_nope.reshape(B, nb, Q_BLOCK, N_HEADS_B, QK_NOPE_DIM).transpose(1, 0, 2, 3, 4)
    qr_b = q_rope.reshape(B, nb, Q_BLOCK, N_HEADS_B, QK_ROPE_DIM).transpose(1, 0, 2, 3, 4)
    starts = jnp.arange(nb, dtype=jnp.int32) * Q_BLOCK
    scale = (QK_NOPE_DIM + QK_ROPE_DIM) ** -0.5
    kpos = jnp.arange(S, dtype=jnp.int32)

    def attend(args):
        qn, qr, start = args
        s = (jnp.einsum('bihd,bjhd->bhij', qn, k_nope, preferred_element_type=jnp.float32)
             + jnp.einsum('bihr,bjr->bhij', qr, k_rope, preferred_element_type=jnp.float32)) * scale
        qpos = start + jnp.arange(Q_BLOCK, dtype=jnp.int32)
        s = jnp.where(kpos[None, :] <= qpos[:, None], s, -jnp.inf)
        p = jax.nn.softmax(s, axis=-1)
        return jnp.einsum('bhij,bjhd->bihd', p.astype(v.dtype), v)

    o = lax.map(attend, (qn_b, qr_b, starts))
    o = o.transpose(1, 0, 2, 3, 4).reshape(B, S, N_HEADS_B * V_HEAD_DIM)
    return o @ w_o


def setup_inputs(seed: int = 0) -> dict:
    key = jax.random.key(seed)
    ks = jax.random.split(key, 24)
    f32 = jnp.float32

    def w(k, shape, fan_in):
        return jax.random.normal(k, shape, f32) * (fan_in ** -0.5)

    def gain(k, shape):
        return 1.0 + 0.01 * jax.random.normal(k, shape, f32)

    D, F = D_MODEL, D_FF
    return {
        "x": jax.random.normal(ks[0], (BATCH, SEQ, D), f32),
        "ffn_norm1": gain(ks[1], (DEPTH, D)),
        "ffn1_wg": w(ks[2], (DEPTH, D, F), D),
        "ffn1_wu": w(ks[3], (DEPTH, D, F), D),
        "ffn1_wd": w(ks[4], (DEPTH, F, D), F),
        "mix_norm": gain(ks[5], (DEPTH, D)),
        "ffn_norm2": gain(ks[6], (DEPTH, D)),
        "ffn2_wg": w(ks[7], (DEPTH, D, F), D),
        "ffn2_wu": w(ks[8], (DEPTH, D, F), D),
        "ffn2_wd": w(ks[9], (DEPTH, F, D), F),
        "a_wqkv": w(ks[10], (N_A_LAYERS, D, 3 * N_HEADS_A * HEAD_DIM_A), D),
        "a_wo": w(ks[11], (N_A_LAYERS, N_HEADS_A * HEAD_DIM_A, D), N_HEADS_A * HEAD_DIM_A),
        "kv_norm": gain(ks[12], (D,)),
        "b_wdkv": w(ks[13], (D, KV_LORA_RANK), D),
        "b_ckv_norm": gain(ks[14], (KV_LORA_RANK,)),
        "b_wkr": w(ks[15], (D, QK_ROPE_DIM), D),
        "b_wuk": w(ks[16], (KV_LORA_RANK, N_HEADS_B, QK_NOPE_DIM), KV_LORA_RANK),
        "b_wuv": w(ks[17], (KV_LORA_RANK, N_HEADS_B, V_HEAD_DIM), KV_LORA_RANK),
        "b_wdq": w(ks[18], (N_B_LAYERS, D, Q_LORA_RANK), D),
        "b_cq_norm": gain(ks[19], (N_B_LAYERS, Q_LORA_RANK)),
        "b_wuq": w(ks[20], (N_B_LAYERS, Q_LORA_RANK, N_HEADS_B, QK_NOPE_DIM + QK_ROPE_DIM), Q_LORA_RANK),
        "b_wo": w(ks[21], (N_B_LAYERS, N_HEADS_B * V_HEAD_DIM, D), N_HEADS_B * V_HEAD_DIM),
        "final_norm": gain(ks[22], (D,)),
    }


def reference(x, ffn_norm1, ffn1_wg, ffn1_wu, ffn1_wd, mix_norm, ffn_norm2, ffn2_wg, ffn2_wu, ffn2_wd,
              a_wqkv, a_wo, kv_norm, b_wdkv, b_ckv_norm, b_wkr, b_wuk, b_wuv,
              b_wdq, b_cq_norm, b_wuq, b_wo, final_norm):
    S = x.shape[1]
    slopes = alibi_slopes(N_HEADS_A)
    cos, sin = rope_tables(S)
    k_nope = k_rope = v_shared = None
    for layer in range(DEPTH):
        if layer == N_A_LAYERS:
            k_nope, k_rope, v_shared = mla_shared_kv(x, kv_norm, b_wdkv, b_ckv_norm, b_wkr, b_wuk, b_wuv, cos, sin)
        x = x + 0.5 * swiglu(rms_norm(x, ffn_norm1[layer]), ffn1_wg[layer], ffn1_wu[layer], ffn1_wd[layer])
        h = rms_norm(x, mix_norm[layer])
        if layer < N_A_LAYERS:
            x = x + dilated_attention(h, a_wqkv[layer], a_wo[layer], slopes)
        else:
            jb = layer - N_A_LAYERS
            x = x + mla_attention(h, k_nope, k_rope, v_shared, b_wdq[jb], b_cq_norm[jb], b_wuq[jb], b_wo[jb], cos, sin)
        x = x + 0.5 * swiglu(rms_norm(x, ffn_norm2[layer]), ffn2_wg[layer], ffn2_wu[layer], ffn2_wd[layer])
    return rms_norm(x, final_norm)
```

```python
import functools

import numpy as np
import jax
import jax.numpy as jnp
from jax import lax
from jax.experimental import pallas as pl
from jax.experimental.pallas import tpu as pltpu

F32 = jnp.float32
BF16 = jnp.bfloat16

D_MODEL = 2048
SEQ = 8192
DEPTH = 4
N_A_LAYERS = DEPTH // 2
EPS = 1e-6
D_FF = 5632
HEAD_DIM = 128
N_HEADS = 16
KV_LORA = 512
Q_LORA = 512
ROPE_DIM = 64
ROPE_THETA = 10000.0
QK_PAD = 256
WINDOW_KEYS = 128
RESIDUES = 16
NEG = -1e30

VMEM_LIMIT_BYTES = 56 * 1024 * 1024


def _params(*semantics):
    return pltpu.CompilerParams(dimension_semantics=semantics, vmem_limit_bytes=VMEM_LIMIT_BYTES)


def _rms(x, g):
    return x * lax.rsqrt(jnp.mean(x * x, axis=-1, keepdims=True) + EPS) * g


def _ffn_kernel(x_ref, g_ref, wg_ref, wu_ref, wd_ref, *rest, final):
    if final:
        fn_ref, o_ref, h_ref = rest
    else:
        o_ref, h_ref = rest
    f = pl.program_id(1)

    @pl.when(f == 0)
    def _():
        x = x_ref[...]
        h_ref[...] = _rms(x, g_ref[...]).astype(BF16)
        o_ref[...] = x

    h = h_ref[...]
    gate = jnp.dot(h, wg_ref[...], preferred_element_type=F32)
    up = jnp.dot(h, wu_ref[...], preferred_element_type=F32)
    act = (0.5 * (gate * jax.nn.sigmoid(gate)) * up).astype(BF16)
    o_ref[...] += jnp.dot(act, wd_ref[...], preferred_element_type=F32)

    if final:
        @pl.when(f == pl.num_programs(1) - 1)
        def _():
            o_ref[...] = _rms(o_ref[...], fn_ref[...])


def _ffn(x, gain, wg, wu, wd, layer, final_gain=None, tm=512, tf=512):
    S, D = x.shape
    F = wg.shape[-1]
    final = final_gain is not None
    in_specs = [
        pl.BlockSpec((tm, D), lambda i, f: (i, 0)),
        pl.BlockSpec((None, 1, D), lambda i, f: (layer, 0, 0)),
        pl.BlockSpec((None, D, tf), lambda i, f: (layer, 0, f)),
        pl.BlockSpec((None, D, tf), lambda i, f: (layer, 0, f)),
        pl.BlockSpec((None, tf, D), lambda i, f: (layer, f, 0)),
    ]
    args = [x, gain, wg, wu, wd]
    if final:
        in_specs.append(pl.BlockSpec((1, D), lambda i, f: (0, 0)))
        args.append(final_gain)
    return pl.pallas_call(
        functools.partial(_ffn_kernel, final=final),
        out_shape=jax.ShapeDtypeStruct((S, D), F32),
        grid=(S // tm, F // tf),
        in_specs=in_specs,
        out_specs=pl.BlockSpec((tm, D), lambda i, f: (i, 0)),
        scratch_shapes=[pltpu.VMEM((tm, D), BF16)],
        compiler_params=_params("parallel", "arbitrary"),
        name="ffn_final" if final else "ffn",
    )(*args)


def _norm_proj_kernel(x_ref, g_ref, w_ref, o_ref, h_ref, *, slabs):
    @pl.when(pl.program_id(1) == 0)
    def _():
        h_ref[...] = _rms(x_ref[...], g_ref[...]).astype(BF16)

    y = jnp.dot(h_ref[...], w_ref[...], preferred_element_type=F32)
    if slabs:
        for s in range(slabs):
            o_ref[s] = y[:, s * HEAD_DIM:(s + 1) * HEAD_DIM]
    else:
        o_ref[...] = y


def _norm_proj(x, gain, gain_idx, w, w_idx, tn, slab_out, tm=512):
    S, D = x.shape
    N = w.shape[-1]
    if slab_out:
        slabs = tn // HEAD_DIM
        out_shape = jax.ShapeDtypeStruct((N // HEAD_DIM, S, HEAD_DIM), F32)
        out_spec = pl.BlockSpec((slabs, tm, HEAD_DIM), lambda i, j: (j, i, 0))
    else:
        slabs = 0
        out_shape = jax.ShapeDtypeStruct((S, N), F32)
        out_spec = pl.BlockSpec((tm, tn), lambda i, j: (i, j))
    return pl.pallas_call(
        functools.partial(_norm_proj_kernel, slabs=slabs),
        out_shape=out_shape,
        grid=(S // tm, N // tn),
        in_specs=[
            pl.BlockSpec((tm, D), lambda i, j: (i, 0)),
            pl.BlockSpec((None, 1, D), lambda i, j: (gain_idx, 0, 0)),
            pl.BlockSpec((None, D, tn), lambda i, j: (w_idx, 0, j)),
        ],
        out_specs=out_spec,
        scratch_shapes=[pltpu.VMEM((tm, D), BF16)],
        compiler_params=_params("parallel", "arbitrary"),
        name="norm_proj_slab" if slab_out else "norm_proj",
    )(x, gain, w)


def _proj_res_kernel(a_ref, w_ref, x_ref, o_ref):
    o_ref[...] = x_ref[...] + jnp.dot(a_ref[...], w_ref[...], preferred_element_type=F32)


def _proj_res(a, w, w_idx, x, tm=512):
    S, D = x.shape
    K = a.shape[-1]
    return pl.pallas_call(
        _proj_res_kernel,
        out_shape=jax.ShapeDtypeStruct((S, D), F32),
        grid=(S // tm,),
        in_specs=[
            pl.BlockSpec((tm, K), lambda i: (i, 0)),
            pl.BlockSpec((None, K, D), lambda i: (w_idx, 0, 0)),
            pl.BlockSpec((tm, D), lambda i: (i, 0)),
        ],
        out_specs=pl.BlockSpec((tm, D), lambda i: (i, 0)),
        compiler_params=_params("parallel"),
        name="proj_res",
    )(a, w, x)


def _dil_bias(slope, dilation, steps):
    valid = (steps >= 0) & (steps <= WINDOW_KEYS)
    return jnp.where(valid, (-slope * dilation) * steps.astype(F32), NEG)


def _dil_tile(q_ref, k_ref, v_ref, acc_ref, m_ref, l_ref, q_chunks, k_chunks, bias, first):
    def gather(ref, chunks):
        parts = [ref[pl.ds(start, size), :] for start, size in chunks]
        return parts[0] if len(parts) == 1 else jnp.concatenate(parts, axis=0)

    q = (gather(q_ref, q_chunks) * (HEAD_DIM ** -0.5)).astype(BF16)
    k = gather(k_ref, k_chunks).astype(BF16)
    v = gather(v_ref, k_chunks).astype(BF16)
    s = lax.dot_general(q, k, (((1,), (1,)), ((), ())), preferred_element_type=F32) + bias
    m_tile = jnp.max(s, axis=-1, keepdims=True)
    if first:
        p = jnp.exp(s - m_tile)
        m_new = jnp.broadcast_to(m_tile, (WINDOW_KEYS, HEAD_DIM))
        l_new = jnp.broadcast_to(jnp.sum(p, axis=-1, keepdims=True), (WINDOW_KEYS, HEAD_DIM))
        acc_new = jnp.dot(p.astype(BF16), v, preferred_element_type=F32)
    else:
        m_old = gather(m_ref, q_chunks)
        m_new = jnp.maximum(m_old, m_tile)
        alpha = jnp.exp(m_old - m_new)
        reps = s.shape[1] // HEAD_DIM
        m_wide = m_new if reps == 1 else jnp.concatenate([m_new] * reps, axis=1)
        p = jnp.exp(s - m_wide)
        l_new = alpha * gather(l_ref, q_chunks) + jnp.sum(p, axis=-1, keepdims=True)
        acc_new = alpha * gather(acc_ref, q_chunks) + jnp.dot(p.astype(BF16), v, preferred_element_type=F32)
    off = 0
    for start, size in q_chunks:
        m_ref[pl.ds(start, size), :] = m_new[off:off + size]
        l_ref[pl.ds(start, size), :] = l_new[off:off + size]
        acc_ref[pl.ds(start, size), :] = acc_new[off:off + size]
        off += size


def _dilated_kernel(slopes_ref, q_ref, k_ref, v_ref, o_ref, acc_ref, m_ref, l_ref):
    S = q_ref.shape[0]
    A = S // RESIDUES
    n = WINDOW_KEYS
    slope = slopes_ref[pl.program_id(0)]
    tile = functools.partial(_dil_tile, q_ref, k_ref, v_ref, acc_ref, m_ref, l_ref)

    def iota(shape, dim):
        return lax.broadcasted_iota(jnp.int32, shape, dim)

    def al(x):
        return x if isinstance(x, int) else pl.multiple_of(x, 8)

    qi, kj = iota((n, 2 * n), 0), iota((n, 2 * n), 1)
    steps = n + 16 * ((qi & 7) - (kj & 15)) + ((qi >> 3) - (kj >> 4))
    bias1 = _dil_bias(slope, 1.0, steps)
    qi, kj = iota((n, n), 0), iota((n, n), 1)
    steps = 16 * ((qi & 7) - (kj & 7)) + ((qi >> 3) - (kj >> 3))
    bias1_first = _dil_bias(slope, 1.0, steps)

    def b1(c, has_prev):
        qc = [(al(r * A + c * 8), 8) for r in range(RESIDUES)]
        if has_prev:
            kc = [(al(r * A + (c - 1) * 8), 16) for r in range(RESIDUES)]
        else:
            kc = qc
        tile(qc, kc, bias1 if has_prev else bias1_first, True)

    b1(0, False)

    def b1_body(c, carry):
        b1(c, True)
        return carry

    lax.fori_loop(1, S // n, b1_body, 0, unroll=3)

    qi, kj = iota((n, 2 * n), 0), iota((n, 2 * n), 1)
    steps = n + 4 * ((qi & 31) - (kj & 63)) + ((qi >> 5) - (kj >> 6))
    bias2 = _dil_bias(slope, 4.0, steps)
    qi, kj = iota((n, n), 0), iota((n, n), 1)
    steps = 4 * ((qi & 31) - (kj & 31)) + ((qi >> 5) - (kj >> 5))
    bias2_first = _dil_bias(slope, 4.0, steps)

    def b2(r4, c, has_prev):
        qc = [(al((r4 + 4 * s) * A + c * 32), 32) for s in range(4)]
        if has_prev:
            kc = [(al((r4 + 4 * s) * A + (c - 1) * 32), 64) for s in range(4)]
        else:
            kc = qc
        tile(qc, kc, bias2 if has_prev else bias2_first, False)

    def b2_outer(r4, carry):
        b2(r4, 0, False)

        def b2_body(c, carry2):
            b2(r4, c, True)
            return carry2

        lax.fori_loop(1, S // (4 * n), b2_body, 0, unroll=3)
        return carry

    lax.fori_loop(0, 4, b2_outer, 0)

    qi, kj = iota((n, 2 * n), 0), iota((n, 2 * n), 1)
    bias3 = _dil_bias(slope, 16.0, n + qi - kj)
    qi, kj = iota((n, n), 0), iota((n, n), 1)
    bias3_first = _dil_bias(slope, 16.0, qi - kj)

    def b3_body(r, carry):
        base = r * A
        tile([(al(base), n)], [(al(base), n)], bias3_first, False)
        for c in range(1, A // n):
            tile([(al(base + c * n), n)], [(al(base + (c - 1) * n), 2 * n)], bias3, False)
        return carry

    lax.fori_loop(0, RESIDUES, b3_body, 0)

    o_ref[...] = (acc_ref[...] / l_ref[...]).astype(o_ref.dtype)


def _dilated_attention(qkv, slopes):
    S = qkv.shape[1]
    H = N_HEADS

    def slab(offset):
        return pl.BlockSpec((None, S, HEAD_DIM), lambda h: (offset + h, 0, 0))

    return pl.pallas_call(
        _dilated_kernel,
        out_shape=jax.ShapeDtypeStruct((S, H * HEAD_DIM), BF16),
        grid=(H,),
        in_specs=[pl.BlockSpec(memory_space=pltpu.SMEM), slab(0), slab(H), slab(2 * H)],
        out_specs=pl.BlockSpec((S, HEAD_DIM), lambda h: (0, h)),
        scratch_shapes=[pltpu.VMEM((S, HEAD_DIM), F32)] * 3,
        compiler_params=_params("parallel"),
        name="dilated_attn",
    )(slopes, qkv, qkv, qkv)


def _rope_pad(z, cos_ref, sin_ref):
    return z * cos_ref[...] + pltpu.roll(z, ROPE_DIM, axis=1) * sin_ref[...]


def _mla_q_kernel(t_ref, g_ref, w_ref, cos_ref, sin_ref, o_ref, cq_ref, *, heads, scale):
    @pl.when(pl.program_id(1) == 0)
    def _():
        cq_ref[...] = _rms(t_ref[...], g_ref[...]).astype(BF16)

    y = jnp.dot(cq_ref[...], w_ref[...], preferred_element_type=F32)
    for hh in range(heads):
        base = hh * QK_PAD
        o_ref[hh, :, :HEAD_DIM] = (y[:, base:base + HEAD_DIM] * scale).astype(BF16)
        z = y[:, base + HEAD_DIM:base + QK_PAD]
        o_ref[hh, :, HEAD_DIM:] = (_rope_pad(z, cos_ref, sin_ref) * scale).astype(BF16)


def _mla_q(t, gain, gain_idx, w, w_idx, cos_pad, sin_pad, scale, tm=512, heads=4):
    S, C = t.shape
    return pl.pallas_call(
        functools.partial(_mla_q_kernel, heads=heads, scale=scale),
        out_shape=jax.ShapeDtypeStruct((N_HEADS, S, QK_PAD), BF16),
        grid=(S // tm, N_HEADS // heads),
        in_specs=[
            pl.BlockSpec((tm, C), lambda i, j: (i, 0)),
            pl.BlockSpec((None, 1, C), lambda i, j: (gain_idx, 0, 0)),
            pl.BlockSpec((None, C, heads * QK_PAD), lambda i, j: (w_idx, 0, j)),
            pl.BlockSpec((tm, HEAD_DIM), lambda i, j: (i, 0)),
            pl.BlockSpec((tm, HEAD_DIM), lambda i, j: (i, 0)),
        ],
        out_specs=pl.BlockSpec((heads, tm, QK_PAD), lambda i, j: (j, i, 0)),
        scratch_shapes=[pltpu.VMEM((tm, C), BF16)],
        compiler_params=_params("parallel", "arbitrary"),
        name="mla_q",
    )(t, gain, w, cos_pad, sin_pad)


def _mla_kv_kernel(t_ref, g_ref, wuk_ref, wuv_ref, cos_ref, sin_ref, k_ref, v_ref):
    ckv = _rms(t_ref[:, :KV_LORA], g_ref[...]).astype(BF16)
    k_nope = jnp.dot(ckv, wuk_ref[...], preferred_element_type=F32)
    v_ref[...] = jnp.dot(ckv, wuv_ref[...], preferred_element_type=F32).astype(BF16)
    k_rope = _rope_pad(t_ref[:, KV_LORA:], cos_ref, sin_ref).astype(BF16)
    for h in range(N_HEADS):
        k_ref[h, :, :HEAD_DIM] = k_nope[:, h * HEAD_DIM:(h + 1) * HEAD_DIM].astype(BF16)
        k_ref[h, :, HEAD_DIM:] = k_rope


def _mla_kv(t, gain, wuk, wuv, cos_pad, sin_pad, tm=512):
    S, W = t.shape
    HD = N_HEADS * HEAD_DIM
    return pl.pallas_call(
        _mla_kv_kernel,
        out_shape=(jax.ShapeDtypeStruct((N_HEADS, S, QK_PAD), BF16),
                   jax.ShapeDtypeStruct((S, HD), BF16)),
        grid=(S // tm,),
        in_specs=[
            pl.BlockSpec((tm, W), lambda i: (i, 0)),
            pl.BlockSpec((1, KV_LORA), lambda i: (0, 0)),
            pl.BlockSpec((KV_LORA, HD), lambda i: (0, 0)),
            pl.BlockSpec((KV_LORA, HD), lambda i: (0, 0)),
            pl.BlockSpec((tm, HEAD_DIM), lambda i: (i, 0)),
            pl.BlockSpec((tm, HEAD_DIM), lambda i: (i, 0)),
        ],
        out_specs=(pl.BlockSpec((N_HEADS, tm, QK_PAD), lambda i: (0, i, 0)),
                   pl.BlockSpec((tm, HD), lambda i: (i, 0))),
        compiler_params=_params("parallel"),
        name="mla_kv",
    )(t, gain, wuk, wuv, cos_pad, sin_pad)


def _mla_attn_kernel(qi_ref, kj_ref, q_ref, k_ref, v_ref, o_ref, m_ref, l_ref, acc_ref):
    t = pl.program_id(1)
    i = qi_ref[t]
    j = kj_ref[t]

    @pl.when(j == 0)
    def _():
        m_ref[...] = jnp.full(m_ref.shape, NEG, F32)
        l_ref[...] = jnp.zeros(l_ref.shape, F32)
        acc_ref[...] = jnp.zeros(acc_ref.shape, F32)

    s = lax.dot_general(q_ref[...], k_ref[...], (((1,), (1,)), ((), ())), preferred_element_type=F32)

    def update(s):
        m_old = m_ref[...]
        m_new = jnp.maximum(m_old, jnp.max(s, axis=-1, keepdims=True))
        alpha = jnp.exp(m_old - m_new)
        p = jnp.exp(s - jnp.concatenate([m_new] * (s.shape[1] // HEAD_DIM), axis=1))
        l_ref[...] = alpha * l_ref[...] + jnp.sum(p, axis=-1, keepdims=True)
        acc_ref[...] = alpha * acc_ref[...] + jnp.dot(p.astype(BF16), v_ref[...], preferred_element_type=F32)
        m_ref[...] = m_new

    @pl.when(j < i)
    def _():
        update(s)

    @pl.when(j == i)
    def _():
        row = lax.broadcasted_iota(jnp.int32, s.shape, 0)
        col = lax.broadcasted_iota(jnp.int32, s.shape, 1)
        update(jnp.where(col <= row, s, NEG))
        o_ref[...] = (acc_ref[...] / l_ref[...]).astype(o_ref.dtype)


def _mla_attention(q, k, v, tq=1024):
    H, S, _ = q.shape
    nq = S // tq
    pairs = [(i, j) for i in range(nq) for j in range(i + 1)]
    qi = jnp.asarray(np.array([p[0] for p in pairs], np.int32))
    kj = jnp.asarray(np.array([p[1] for p in pairs], np.int32))
    grid_spec = pltpu.PrefetchScalarGridSpec(
        num_scalar_prefetch=2,
        grid=(H, len(pairs)),
        in_specs=[
            pl.BlockSpec((None, tq, QK_PAD), lambda h, t, qi, kj: (h, qi[t], 0)),
            pl.BlockSpec((None, tq, QK_PAD), lambda h, t, qi, kj: (h, kj[t], 0)),
            pl.BlockSpec((tq, HEAD_DIM), lambda h, t, qi, kj: (kj[t], h)),
        ],
        out_specs=pl.BlockSpec((tq, HEAD_DIM), lambda h, t, qi, kj: (qi[t], h)),
        scratch_shapes=[pltpu.VMEM((tq, HEAD_DIM), F32)] * 3,
    )
    return pl.pallas_call(
        _mla_attn_kernel,
        out_shape=jax.ShapeDtypeStruct((S, H * HEAD_DIM), BF16),
        grid_spec=grid_spec,
        compiler_params=_params("parallel", "arbitrary"),
        name="mla_attn",
    )(qi, kj, q, k, v)


def _rope_tables(seq):
    inv = 1.0 / (ROPE_THETA ** (jnp.arange(0, ROPE_DIM, 2, dtype=F32) / ROPE_DIM))
    ang = jnp.arange(seq, dtype=F32)[:, None] * inv[None, :]
    cos, sin = jnp.cos(ang), jnp.sin(ang)
    zeros = jnp.zeros((seq, HEAD_DIM - ROPE_DIM), F32)
    cos_pad = jnp.concatenate([cos, cos, zeros], axis=1)
    sin_pad = jnp.concatenate([-sin, sin, zeros], axis=1)
    return cos_pad, sin_pad


def kernel(x, ffn_norm1, ffn1_wg, ffn1_wu, ffn1_wd, mix_norm, ffn_norm2, ffn2_wg, ffn2_wu, ffn2_wd,
           a_wqkv, a_wo, kv_norm, b_wdkv, b_ckv_norm, b_wkr, b_wuk, b_wuv,
           b_wdq, b_cq_norm, b_wuq, b_wo, final_norm):
    B, S, D = x.shape
    assert (B, S, D) == (1, SEQ, D_MODEL)
    bf = lambda w: w.astype(BF16)
    gains = lambda g: g.reshape(g.shape[0], 1, g.shape[-1])

    w1g, w1u, w1d = bf(ffn1_wg), bf(ffn1_wu), bf(ffn1_wd)
    w2g, w2u, w2d = bf(ffn2_wg), bf(ffn2_wu), bf(ffn2_wd)
    wqkv, wo_a, wo_b = bf(a_wqkv), bf(a_wo), bf(b_wo)
    g1, g2, gm = gains(ffn_norm1), gains(ffn_norm2), gains(mix_norm)

    rot = (np.arange(ROPE_DIM) + ROPE_DIM // 2) % ROPE_DIM
    w_kv_down = bf(jnp.concatenate([b_wdkv, b_wkr, b_wkr[:, rot]], axis=1))[None]
    wuk = bf(b_wuk.reshape(KV_LORA, N_HEADS * HEAD_DIM))
    wuv = bf(b_wuv.reshape(KV_LORA, N_HEADS * HEAD_DIM))
    wdq = bf(b_wdq)
    q_rope_w = b_wuq[..., HEAD_DIM:]
    wuq = bf(jnp.concatenate([b_wuq, q_rope_w[..., rot]], axis=-1).reshape(
        b_wuq.shape[0], Q_LORA, N_HEADS * QK_PAD))

    slopes = jnp.asarray(2.0 ** (-8.0 * (np.arange(N_HEADS) + 1) / N_HEADS), dtype=F32)
    cos_pad, sin_pad = _rope_tables(S)
    scale = float((HEAD_DIM + ROPE_DIM) ** -0.5)

    xs = x[0]
    xs = xs.reshape(S // RESIDUES, RESIDUES, D).transpose(1, 0, 2).reshape(S, D)
    for layer in range(N_A_LAYERS):
        xs = _ffn(xs, g1, w1g, w1u, w1d, layer)
        qkv = _norm_proj(xs, gm, layer, wqkv, layer, tn=1024, slab_out=True)
        o = _dilated_attention(qkv, slopes)
        xs = _proj_res(o, wo_a, layer, xs)
        xs = _ffn(xs, g2, w2g, w2u, w2d, layer)
    xs = xs.reshape(RESIDUES, S // RESIDUES, D).transpose(1, 0, 2).reshape(S, D)

    t_kv = _norm_proj(xs, kv_norm.reshape(1, 1, D), 0, w_kv_down, 0, tn=w_kv_down.shape[-1], slab_out=False)
    k_all, v_all = _mla_kv(t_kv, b_ckv_norm.reshape(1, KV_LORA), wuk, wuv, cos_pad, sin_pad)
    gq = gains(b_cq_norm)
    for layer in range(N_A_LAYERS, DEPTH):
        jb = layer - N_A_LAYERS
        xs = _ffn(xs, g1, w1g, w1u, w1d, layer)
        t_q = _norm_proj(xs, gm, layer, wdq, jb, tn=Q_LORA, slab_out=False)
        q_all = _mla_q(t_q, gq, jb, wuq, jb, cos_pad, sin_pad, scale)
        o = _mla_attention(q_all, k_all, v_all)
        xs = _proj_res(o, wo_b, jb, xs)
        last = layer == DEPTH - 1
        xs = _ffn(xs, g2, w2g, w2u, w2d, layer, final_gain=final_norm.reshape(1, D) if last else None)
    return xs[None]
```

```python
import functools

import numpy as np
import jax
import jax.numpy as jnp
from jax import lax
from jax.experimental import pallas as pl
from jax.experimental.pallas import tpu as pltpu

F32 = jnp.float32
BF16 = jnp.bfloat16

D_MODEL = 2048
SEQ = 8192
DEPTH = 4
N_A_LAYERS = DEPTH // 2
EPS = 1e-6
D_FF = 5632
HEAD_DIM = 128
N_HEADS = 16
KV_LORA = 512
Q_LORA = 512
ROPE_DIM = 64
ROPE_THETA = 10000.0
QK_PAD = 256
WINDOW_KEYS = 128
RESIDUES = 16
NEG = -1e30

VMEM_LIMIT_BYTES = 56 * 1024 * 1024


def _params(*semantics):
    return pltpu.CompilerParams(dimension_semantics=semantics, vmem_limit_bytes=VMEM_LIMIT_BYTES)


def _rms(x, g):
    return x * lax.rsqrt(jnp.mean(x * x, axis=-1, keepdims=True) + EPS) * g


def _ffn_kernel(x_ref, g_ref, wg_ref, wu_ref, wd_ref, *rest, final):
    if final:
        fn_ref, o_ref, h_ref = rest
    else:
        o_ref, h_ref = rest
    f = pl.program_id(1)

    @pl.when(f == 0)
    def _():
        x = x_ref[...]
        h_ref[...] = _rms(x, g_ref[...]).astype(BF16)
        o_ref[...] = x

    h = h_ref[...]
    gate = jnp.dot(h, wg_ref[...].astype(BF16), preferred_element_type=F32)
    up = jnp.dot(h, wu_ref[...].astype(BF16), preferred_element_type=F32)
    act = (0.5 * (gate * jax.nn.sigmoid(gate)) * up).astype(BF16)
    o_ref[...] += jnp.dot(act, wd_ref[...].astype(BF16), preferred_element_type=F32)

    if final:
        @pl.when(f == pl.num_programs(1) - 1)
        def _():
            o_ref[...] = _rms(o_ref[...], fn_ref[...])


def _ffn(x, gain, wg, wu, wd, layer, final_gain=None, tm=1024, tf=256):
    S, D = x.shape
    F = wg.shape[-1]
    final = final_gain is not None
    in_specs = [
        pl.BlockSpec((tm, D), lambda i, f: (i, 0)),
        pl.BlockSpec((None, 1, D), lambda i, f: (layer, 0, 0)),
        pl.BlockSpec((None, D, tf), lambda i, f: (layer, 0, f)),
        pl.BlockSpec((None, D, tf), lambda i, f: (layer, 0, f)),
        pl.BlockSpec((None, tf, D), lambda i, f: (layer, f, 0)),
    ]
    args = [x, gain, wg, wu, wd]
    if final:
        in_specs.append(pl.BlockSpec((1, D), lambda i, f: (0, 0)))
        args.append(final_gain)
    return pl.pallas_call(
        functools.partial(_ffn_kernel, final=final),
        out_shape=jax.ShapeDtypeStruct((S, D), F32),
        grid=(S // tm, F // tf),
        in_specs=in_specs,
        out_specs=pl.BlockSpec((tm, D), lambda i, f: (i, 0)),
        scratch_shapes=[pltpu.VMEM((tm, D), BF16)],
        compiler_params=_params("parallel", "arbitrary"),
        name="ffn_final" if final else "ffn",
    )(*args)


def _norm_proj_kernel(x_ref, g_ref, w_ref, o_ref, h_ref, *, slabs):
    @pl.when(pl.program_id(1) == 0)
    def _():
        h_ref[...] = _rms(x_ref[...], g_ref[...]).astype(BF16)

    y = jnp.dot(h_ref[...], w_ref[...].astype(BF16), preferred_element_type=F32)
    if slabs:
        for s in range(slabs):
            o_ref[s] = y[:, s * HEAD_DIM:(s + 1) * HEAD_DIM]
    else:
        o_ref[...] = y


def _norm_proj(x, gain, gain_idx, w, w_idx, tn, slab_out, tm=1024):
    S, D = x.shape
    N = w.shape[-1]
    if slab_out:
        slabs = tn // HEAD_DIM
        out_shape = jax.ShapeDtypeStruct((N // HEAD_DIM, S, HEAD_DIM), F32)
        out_spec = pl.BlockSpec((slabs, tm, HEAD_DIM), lambda i, j: (j, i, 0))
    else:
        slabs = 0
        out_shape = jax.ShapeDtypeStruct((S, N), F32)
        out_spec = pl.BlockSpec((tm, tn), lambda i, j: (i, j))
    return pl.pallas_call(
        functools.partial(_norm_proj_kernel, slabs=slabs),
        out_shape=out_shape,
        grid=(S // tm, N // tn),
        in_specs=[
            pl.BlockSpec((tm, D), lambda i, j: (i, 0)),
            pl.BlockSpec((None, 1, D), lambda i, j: (gain_idx, 0, 0)),
            pl.BlockSpec((None, D, tn), lambda i, j: (w_idx, 0, j)),
        ],
        out_specs=out_spec,
        scratch_shapes=[pltpu.VMEM((tm, D), BF16)],
        compiler_params=_params("parallel", "arbitrary"),
        name="norm_proj_slab" if slab_out else "norm_proj",
    )(x, gain, w)


def _proj_res_kernel(a_ref, w_ref, x_ref, o_ref):
    o_ref[...] = x_ref[...] + jnp.dot(a_ref[...], w_ref[...].astype(BF16), preferred_element_type=F32)


def _proj_res(a, w, w_idx, x, tm=1024, tn=512):
    S, D = x.shape
    K = a.shape[-1]
    return pl.pallas_call(
        _proj_res_kernel,
        out_shape=jax.ShapeDtypeStruct((S, D), F32),
        grid=(S // tm, D // tn),
        in_specs=[
            pl.BlockSpec((tm, K), lambda i, j: (i, 0)),
            pl.BlockSpec((None, K, tn), lambda i, j: (w_idx, 0, j)),
            pl.BlockSpec((tm, tn), lambda i, j: (i, j)),
        ],
        out_specs=pl.BlockSpec((tm, tn), lambda i, j: (i, j)),
        compiler_params=_params("parallel", "arbitrary"),
        name="proj_res",
    )(a, w, x)


LOG2E = float(np.log2(np.e))


def _dil_bias(slope, dilation, steps):
    valid = (steps >= 0) & (steps <= WINDOW_KEYS)
    return jnp.where(valid, (-slope * dilation * LOG2E) * steps.astype(F32), NEG)


def _gather_rows(ref, chunks):
    parts = [ref[pl.ds(start, size), :] for start, size in chunks]
    return parts[0] if len(parts) == 1 else jnp.concatenate(parts, axis=0)


def _dil_tiles(q_ref, k_ref, v_ref, acc_ref, m_ref, tiles, first):
    old = []
    for q_chunks, _, _ in tiles:
        old.append(None if first else (_gather_rows(m_ref, q_chunks), _gather_rows(acc_ref, q_chunks)))
    new = []
    for (q_chunks, k_chunks, bias), state in zip(tiles, old):
        q = (_gather_rows(q_ref, q_chunks) * (HEAD_DIM ** -0.5 * LOG2E)).astype(BF16)
        k = _gather_rows(k_ref, k_chunks).astype(BF16)
        v = _gather_rows(v_ref, k_chunks).astype(BF16)
        v1 = jnp.concatenate([v, jnp.ones_like(v)], axis=1)
        s = lax.dot_general(q, k, (((1,), (1,)), ((), ())), preferred_element_type=F32) + bias
        m_tile = jnp.max(s, axis=-1, keepdims=True)
        if first:
            m_new = jnp.broadcast_to(m_tile, (WINDOW_KEYS, HEAD_DIM))
            p = jnp.exp2(s - m_tile)
            acc_new = jnp.dot(p.astype(BF16), v1, preferred_element_type=F32)
        else:
            m_old, acc_old = state
            m_new = jnp.maximum(m_old, m_tile)
            alpha = jnp.exp2(m_old - m_new)
            reps = s.shape[1] // HEAD_DIM
            m_wide = m_new if reps == 1 else jnp.concatenate([m_new] * reps, axis=1)
            p = jnp.exp2(s - m_wide)
            acc_new = (jnp.concatenate([alpha, alpha], axis=1) * acc_old
                       + jnp.dot(p.astype(BF16), v1, preferred_element_type=F32))
        new.append((m_new, acc_new))
    for (q_chunks, _, _), (m_new, acc_new) in zip(tiles, new):
        off = 0
        for start, size in q_chunks:
            m_ref[pl.ds(start, size), :] = m_new[off:off + size]
            acc_ref[pl.ds(start, size), :] = acc_new[off:off + size]
            off += size


def _dilated_kernel(slopes_ref, q_ref, k_ref, v_ref, o_ref, acc_ref, m_ref, *, batch1, batch2):
    S = q_ref.shape[0]
    A = S // RESIDUES
    n = WINDOW_KEYS
    slope = slopes_ref[pl.program_id(0)]
    run = functools.partial(_dil_tiles, q_ref, k_ref, v_ref, acc_ref, m_ref)

    def iota(shape, dim):
        return lax.broadcasted_iota(jnp.int32, shape, dim)

    def al(x):
        return x if isinstance(x, int) else pl.multiple_of(x, 8)

    qi, kj = iota((n, 2 * n), 0), iota((n, 2 * n), 1)
    steps = n + 16 * ((qi & 7) - (kj & 15)) + ((qi >> 3) - (kj >> 4))
    bias1 = _dil_bias(slope, 1.0, steps)
    qi, kj = iota((n, n), 0), iota((n, n), 1)
    steps = 16 * ((qi & 7) - (kj & 7)) + ((qi >> 3) - (kj >> 3))
    bias1_first = _dil_bias(slope, 1.0, steps)

    def b1(c, has_prev):
        qc = [(al(r * A + c * 8), 8) for r in range(RESIDUES)]
        if has_prev:
            kc = [(al(r * A + (c - 1) * 8), 16) for r in range(RESIDUES)]
        else:
            kc = qc
        return qc, kc, bias1 if has_prev else bias1_first

    run([b1(0, False)], True)

    def b1_body(g, carry):
        run([b1(1 + g * batch1 + t, True) for t in range(batch1)], True)
        return carry

    lax.fori_loop(0, (S // n - 1) // batch1, b1_body, 0)

    qi, kj = iota((n, 2 * n), 0), iota((n, 2 * n), 1)
    steps = n + 4 * ((qi & 31) - (kj & 63)) + ((qi >> 5) - (kj >> 6))
    bias2 = _dil_bias(slope, 4.0, steps)
    qi, kj = iota((n, n), 0), iota((n, n), 1)
    steps = 4 * ((qi & 31) - (kj & 31)) + ((qi >> 5) - (kj >> 5))
    bias2_first = _dil_bias(slope, 4.0, steps)

    def b2(r4, c, has_prev):
        qc = [(al((r4 + 4 * s) * A + c * 32), 32) for s in range(4)]
        if has_prev:
            kc = [(al((r4 + 4 * s) * A + (c - 1) * 32), 64) for s in range(4)]
        else:
            kc = qc
        return qc, kc, bias2 if has_prev else bias2_first

    def b2_outer(r4, carry):
        def b2_body(g, carry2):
            run([b2(r4, 1 + g * batch2 + t, True) for t in range(batch2)], False)
            return carry2

        run([b2(r4, 0, False)], False)
        lax.fori_loop(0, (S // (4 * n) - 1) // batch2, b2_body, 0)
        return carry

    lax.fori_loop(0, 4, b2_outer, 0)

    qi, kj = iota((n, 2 * n), 0), iota((n, 2 * n), 1)
    bias3 = _dil_bias(slope, 16.0, n + qi - kj)
    qi, kj = iota((n, n), 0), iota((n, n), 1)
    bias3_first = _dil_bias(slope, 16.0, qi - kj)

    def b3_body(g, carry):
        tiles = []
        for r in (2 * g, 2 * g + 1):
            base = r * A
            tiles.append(([(al(base), n)], [(al(base), n)], bias3_first))
            for c in range(1, A // n):
                tiles.append(([(al(base + c * n), n)], [(al(base + (c - 1) * n), 2 * n)], bias3))
        run(tiles, False)
        return carry

    lax.fori_loop(0, RESIDUES // 2, b3_body, 0)

    o_ref[...] = (acc_ref[:, :HEAD_DIM] / acc_ref[:, HEAD_DIM:]).astype(o_ref.dtype)


def _dilated_attention(qkv, slopes):
    S = qkv.shape[1]
    H = N_HEADS

    def slab(offset):
        return pl.BlockSpec((None, S, HEAD_DIM), lambda h: (offset + h, 0, 0))

    return pl.pallas_call(
        functools.partial(_dilated_kernel, batch1=7, batch2=5),
        out_shape=jax.ShapeDtypeStruct((S, H * HEAD_DIM), BF16),
        grid=(H,),
        in_specs=[pl.BlockSpec(memory_space=pltpu.SMEM), slab(0), slab(H), slab(2 * H)],
        out_specs=pl.BlockSpec((S, HEAD_DIM), lambda h: (0, h)),
        scratch_shapes=[pltpu.VMEM((S, 2 * HEAD_DIM), F32), pltpu.VMEM((S, HEAD_DIM), F32)],
        compiler_params=_params("parallel"),
        name="dilated_attn",
    )(slopes, qkv, qkv, qkv)


def _rope_pad(z, cos_ref, sin_ref):
    return z * cos_ref[...] + pltpu.roll(z, ROPE_DIM, axis=1) * sin_ref[...]


def _mla_q_kernel(t_ref, g_ref, w_ref, cos_ref, sin_ref, o_ref, cq_ref, *, heads, scale):
    @pl.when(pl.program_id(1) == 0)
    def _():
        cq_ref[...] = _rms(t_ref[...], g_ref[...]).astype(BF16)

    y = jnp.dot(cq_ref[...], w_ref[...], preferred_element_type=F32)
    for hh in range(heads):
        base = hh * QK_PAD
        o_ref[hh, :, :HEAD_DIM] = (y[:, base:base + HEAD_DIM] * scale).astype(BF16)
        z = y[:, base + HEAD_DIM:base + QK_PAD]
        o_ref[hh, :, HEAD_DIM:] = (_rope_pad(z, cos_ref, sin_ref) * scale).astype(BF16)


def _mla_q(t, gain, gain_idx, w, w_idx, cos_pad, sin_pad, scale, tm=512, heads=4):
    S, C = t.shape
    return pl.pallas_call(
        functools.partial(_mla_q_kernel, heads=heads, scale=scale),
        out_shape=jax.ShapeDtypeStruct((N_HEADS, S, QK_PAD), BF16),
        grid=(S // tm, N_HEADS // heads),
        in_specs=[
            pl.BlockSpec((tm, C), lambda i, j: (i, 0)),
            pl.BlockSpec((None, 1, C), lambda i, j: (gain_idx, 0, 0)),
            pl.BlockSpec((None, C, heads * QK_PAD), lambda i, j: (w_idx, 0, j)),
            pl.BlockSpec((tm, HEAD_DIM), lambda i, j: (i, 0)),
            pl.BlockSpec((tm, HEAD_DIM), lambda i, j: (i, 0)),
        ],
        out_specs=pl.BlockSpec((heads, tm, QK_PAD), lambda i, j: (j, i, 0)),
        scratch_shapes=[pltpu.VMEM((tm, C), BF16)],
        compiler_params=_params("parallel", "arbitrary"),
        name="mla_q",
    )(t, gain, w, cos_pad, sin_pad)


def _mla_kv_kernel(t_ref, g_ref, wuk_ref, wuv_ref, cos_ref, sin_ref, k_ref, v_ref):
    ckv = _rms(t_ref[:, :KV_LORA], g_ref[...]).astype(BF16)
    k_nope = jnp.dot(ckv, wuk_ref[...], preferred_element_type=F32)
    v = jnp.dot(ckv, wuv_ref[...], preferred_element_type=F32)
    k_rope = _rope_pad(t_ref[:, KV_LORA:], cos_ref, sin_ref).astype(BF16)
    ones = jnp.ones((t_ref.shape[0], HEAD_DIM), BF16)
    for h in range(N_HEADS):
        cols = slice(h * HEAD_DIM, (h + 1) * HEAD_DIM)
        k_ref[h, :, :HEAD_DIM] = k_nope[:, cols].astype(BF16)
        k_ref[h, :, HEAD_DIM:] = k_rope
        v_ref[h, :, :HEAD_DIM] = v[:, cols].astype(BF16)
        v_ref[h, :, HEAD_DIM:] = ones


def _mla_kv(t, gain, wuk, wuv, cos_pad, sin_pad, tm=512):
    S, W = t.shape
    HD = N_HEADS * HEAD_DIM
    return pl.pallas_call(
        _mla_kv_kernel,
        out_shape=(jax.ShapeDtypeStruct((N_HEADS, S, QK_PAD), BF16),
                   jax.ShapeDtypeStruct((N_HEADS, S, 2 * HEAD_DIM), BF16)),
        grid=(S // tm,),
        in_specs=[
            pl.BlockSpec((tm, W), lambda i: (i, 0)),
            pl.BlockSpec((1, KV_LORA), lambda i: (0, 0)),
            pl.BlockSpec((KV_LORA, HD), lambda i: (0, 0)),
            pl.BlockSpec((KV_LORA, HD), lambda i: (0, 0)),
            pl.BlockSpec((tm, HEAD_DIM), lambda i: (i, 0)),
            pl.BlockSpec((tm, HEAD_DIM), lambda i: (i, 0)),
        ],
        out_specs=(pl.BlockSpec((N_HEADS, tm, QK_PAD), lambda i: (0, i, 0)),
                   pl.BlockSpec((N_HEADS, tm, 2 * HEAD_DIM), lambda i: (0, i, 0))),
        compiler_params=_params("parallel"),
        name="mla_kv",
    )(t, gain, wuk, wuv, cos_pad, sin_pad)


def _mla_attn_kernel(q_ref, k_ref, v_ref, o_ref, s_ref, bmax_ref, m_ref, acc_ref, *, blk, chains):
    i = pl.program_id(1)
    rows = blk // chains
    nt = (((1,), (1,)), ((), ()))

    def scores(j, slot, masked):
        k = k_ref[pl.ds(pl.multiple_of(j * blk, blk), blk), :]
        for c in range(chains):
            r = slice(c * rows, (c + 1) * rows)
            s = lax.dot_general(q_ref[r, :], k, nt, preferred_element_type=F32)
            if masked:
                row = lax.broadcasted_iota(jnp.int32, s.shape, 0) + c * rows
                col = lax.broadcasted_iota(jnp.int32, s.shape, 1)
                s = jnp.where(col <= row, s, NEG)
            s_ref[slot, r, :] = s
            bmax_ref[slot, r, :] = jnp.broadcast_to(jnp.max(s, axis=-1, keepdims=True), (rows, HEAD_DIM))

    def update(j, slot):
        v = v_ref[pl.ds(pl.multiple_of(j * blk, blk), blk), :]
        for c in range(chains):
            r = slice(c * rows, (c + 1) * rows)
            m_old = m_ref[r, :]
            m_new = jnp.maximum(m_old, bmax_ref[slot, r, :])
            alpha = jnp.exp2(m_old - m_new)
            p = jnp.exp2(s_ref[slot, r, :] - jnp.concatenate([m_new] * (blk // HEAD_DIM), axis=1))
            acc_ref[r, :] = (jnp.concatenate([alpha, alpha], axis=1) * acc_ref[r, :]
                             + jnp.dot(p.astype(BF16), v, preferred_element_type=F32))
            m_ref[r, :] = m_new

    m_ref[...] = jnp.full(m_ref.shape, NEG, F32)
    acc_ref[...] = jnp.zeros(acc_ref.shape, F32)

    @pl.when(i == 0)
    def _():
        scores(0, 0, True)

    @pl.when(i > 0)
    def _():
        scores(0, 0, False)

    def pair(jj, carry):
        j = 2 * jj
        scores(j + 1, 1, False)
        update(j, 0)
        scores(j + 2, 0, False)
        update(j + 1, 1)
        return carry

    lax.fori_loop(0, (i - 1) // 2, pair, 0)

    @pl.when((i > 0) & (i % 2 == 0))
    def _():
        scores(i - 1, 1, False)
        update(i - 2, 0)
        scores(i, 0, True)
        update(i - 1, 1)
        update(i, 0)

    @pl.when(i % 2 == 1)
    def _():
        scores(i, 1, True)
        update(i - 1, 0)
        update(i, 1)

    @pl.when(i == 0)
    def _():
        update(0, 0)

    o_ref[...] = (acc_ref[:, :HEAD_DIM] / acc_ref[:, HEAD_DIM:]).astype(o_ref.dtype)


def _mla_attention(q, k, v, blk=1024, chains=1):
    H, S, _ = q.shape
    return pl.pallas_call(
        functools.partial(_mla_attn_kernel, blk=blk, chains=chains),
        out_shape=jax.ShapeDtypeStruct((S, H * HEAD_DIM), BF16),
        grid=(H, S // blk),
        in_specs=[
            pl.BlockSpec((None, blk, QK_PAD), lambda h, i: (h, i, 0)),
            pl.BlockSpec((None, S, QK_PAD), lambda h, i: (h, 0, 0)),
            pl.BlockSpec((None, S, 2 * HEAD_DIM), lambda h, i: (h, 0, 0)),
        ],
        out_specs=pl.BlockSpec((blk, HEAD_DIM), lambda h, i: (i, h)),
        scratch_shapes=[
            pltpu.VMEM((2, blk, blk), F32),
            pltpu.VMEM((2, blk, HEAD_DIM), F32),
            pltpu.VMEM((blk, HEAD_DIM), F32),
            pltpu.VMEM((blk, 2 * HEAD_DIM), F32),
        ],
        compiler_params=_params("parallel", "arbitrary"),
        name="mla_attn",
    )(q, k, v)


def _rope_tables(seq):
    inv = 1.0 / (ROPE_THETA ** (jnp.arange(0, ROPE_DIM, 2, dtype=F32) / ROPE_DIM))
    ang = jnp.arange(seq, dtype=F32)[:, None] * inv[None, :]
    cos, sin = jnp.cos(ang), jnp.sin(ang)
    zeros = jnp.zeros((seq, HEAD_DIM - ROPE_DIM), F32)
    cos_pad = jnp.concatenate([cos, cos, zeros], axis=1)
    sin_pad = jnp.concatenate([-sin, sin, zeros], axis=1)
    return cos_pad, sin_pad


def kernel(x, ffn_norm1, ffn1_wg, ffn1_wu, ffn1_wd, mix_norm, ffn_norm2, ffn2_wg, ffn2_wu, ffn2_wd,
           a_wqkv, a_wo, kv_norm, b_wdkv, b_ckv_norm, b_wkr, b_wuk, b_wuv,
           b_wdq, b_cq_norm, b_wuq, b_wo, final_norm):
    B, S, D = x.shape
    assert (B, S, D) == (1, SEQ, D_MODEL)
    bf = lambda w: w.astype(BF16)
    gains = lambda g: g.reshape(g.shape[0], 1, g.shape[-1])

    w1g, w1u, w1d = ffn1_wg, ffn1_wu, ffn1_wd
    w2g, w2u, w2d = ffn2_wg, ffn2_wu, ffn2_wd
    wqkv, wo_a, wo_b = a_wqkv, a_wo, b_wo
    g1, g2, gm = gains(ffn_norm1), gains(ffn_norm2), gains(mix_norm)

    rot = (np.arange(ROPE_DIM) + ROPE_DIM // 2) % ROPE_DIM
    w_kv_down = bf(jnp.concatenate([b_wdkv, b_wkr, b_wkr[:, rot]], axis=1))[None]
    wuk = bf(b_wuk.reshape(KV_LORA, N_HEADS * HEAD_DIM))
    wuv = bf(b_wuv.reshape(KV_LORA, N_HEADS * HEAD_DIM))
    wdq = bf(b_wdq)
    q_rope_w = b_wuq[..., HEAD_DIM:]
    wuq = bf(jnp.concatenate([b_wuq, q_rope_w[..., rot]], axis=-1).reshape(
        b_wuq.shape[0], Q_LORA, N_HEADS * QK_PAD))

    slopes = jnp.asarray(2.0 ** (-8.0 * (np.arange(N_HEADS) + 1) / N_HEADS), dtype=F32)
    cos_pad, sin_pad = _rope_tables(S)
    scale = float((HEAD_DIM + ROPE_DIM) ** -0.5 * np.log2(np.e))

    xs = x[0]
    xs = xs.reshape(S // RESIDUES, RESIDUES, D).transpose(1, 0, 2).reshape(S, D)
    for layer in range(N_A_LAYERS):
        xs = _ffn(xs, g1, w1g, w1u, w1d, layer)
        qkv = _norm_proj(xs, gm, layer, wqkv, layer, tn=512, slab_out=True)
        o = _dilated_attention(qkv, slopes)
        xs = _proj_res(o, wo_a, layer, xs)
        xs = _ffn(xs, g2, w2g, w2u, w2d, layer)
    xs = xs.reshape(RESIDUES, S // RESIDUES, D).transpose(1, 0, 2).reshape(S, D)

    t_kv = _norm_proj(xs, kv_norm.reshape(1, 1, D), 0, w_kv_down, 0, tn=w_kv_down.shape[-1], slab_out=False)
    k_all, v_all = _mla_kv(t_kv, b_ckv_norm.reshape(1, KV_LORA), wuk, wuv, cos_pad, sin_pad)
    gq = gains(b_cq_norm)
    for layer in range(N_A_LAYERS, DEPTH):
        jb = layer - N_A_LAYERS
        xs = _ffn(xs, g1, w1g, w1u, w1d, layer)
        t_q = _norm_proj(xs, gm, layer, wdq, jb, tn=Q_LORA, slab_out=False)
        q_all = _mla_q(t_q, gq, jb, wuq, jb, cos_pad, sin_pad, scale)
        o = _mla_attention(q_all, k_all, v_all)
        xs = _proj_res(o, wo_b, jb, xs)
        last = layer == DEPTH - 1
        xs = _ffn(xs, g2, w2g, w2u, w2d, layer, final_gain=final_norm.reshape(1, D) if last else None)
    return xs[None]
```

```python
import functools

import numpy as np
import jax
import jax.numpy as jnp
from jax import lax
from jax.experimental import pallas as pl
from jax.experimental.pallas import tpu as pltpu

F32 = jnp.float32
BF16 = jnp.bfloat16

D_MODEL = 2048
SEQ = 8192
DEPTH = 4
N_A_LAYERS = DEPTH // 2
EPS = 1e-6
D_FF = 5632
HEAD_DIM = 128
N_HEADS = 16
KV_LORA = 512
Q_LORA = 512
ROPE_DIM = 64
ROPE_THETA = 10000.0
QK_PAD = 256
WINDOW_KEYS = 128
RESIDUES = 16
NEG = -1e30

VMEM_LIMIT_BYTES = 56 * 1024 * 1024


def _params(*semantics):
    return pltpu.CompilerParams(dimension_semantics=semantics, vmem_limit_bytes=VMEM_LIMIT_BYTES)


def _rms(x, g):
    return x * lax.rsqrt(jnp.mean(x * x, axis=-1, keepdims=True) + EPS) * g


def _ffn_kernel(x_ref, g_ref, wg_ref, wu_ref, wd_ref, *rest, final):
    if final:
        fn_ref, o_ref, h_ref = rest
    else:
        o_ref, h_ref = rest
    f = pl.program_id(1)

    @pl.when(f == 0)
    def _():
        x = x_ref[...]
        h_ref[...] = _rms(x, g_ref[...]).astype(BF16)
        o_ref[...] = x

    h = h_ref[...]
    gate = jnp.dot(h, wg_ref[...].astype(BF16), preferred_element_type=F32)
    up = jnp.dot(h, wu_ref[...].astype(BF16), preferred_element_type=F32)
    act = (0.5 * (gate * jax.nn.sigmoid(gate)) * up).astype(BF16)
    o_ref[...] += jnp.dot(act, wd_ref[...].astype(BF16), preferred_element_type=F32)

    if final:
        @pl.when(f == pl.num_programs(1) - 1)
        def _():
            o_ref[...] = _rms(o_ref[...], fn_ref[...])


def _ffn(x, gain, wg, wu, wd, layer, final_gain=None, tm=1024, tf=256):
    S, D = x.shape
    F = wg.shape[-1]
    final = final_gain is not None
    in_specs = [
        pl.BlockSpec((tm, D), lambda i, f: (i, 0)),
        pl.BlockSpec((None, 1, D), lambda i, f: (layer, 0, 0)),
        pl.BlockSpec((None, D, tf), lambda i, f: (layer, 0, f)),
        pl.BlockSpec((None, D, tf), lambda i, f: (layer, 0, f)),
        pl.BlockSpec((None, tf, D), lambda i, f: (layer, f, 0)),
    ]
    args = [x, gain, wg, wu, wd]
    if final:
        in_specs.append(pl.BlockSpec((1, D), lambda i, f: (0, 0)))
        args.append(final_gain)
    return pl.pallas_call(
        functools.partial(_ffn_kernel, final=final),
        out_shape=jax.ShapeDtypeStruct((S, D), F32),
        grid=(S // tm, F // tf),
        in_specs=in_specs,
        out_specs=pl.BlockSpec((tm, D), lambda i, f: (i, 0)),
        scratch_shapes=[pltpu.VMEM((tm, D), BF16)],
        compiler_params=_params("parallel", "arbitrary"),
        name="ffn_final" if final else "ffn",
    )(*args)


def _norm_proj_kernel(x_ref, g_ref, w_ref, o_ref, h_ref, *, slabs):
    @pl.when(pl.program_id(1) == 0)
    def _():
        h_ref[...] = _rms(x_ref[...], g_ref[...]).astype(BF16)

    y = jnp.dot(h_ref[...], w_ref[...].astype(BF16), preferred_element_type=F32)
    if slabs:
        for s in range(slabs):
            o_ref[s] = y[:, s * HEAD_DIM:(s + 1) * HEAD_DIM]
    else:
        o_ref[...] = y


def _norm_proj(x, gain, gain_idx, w, w_idx, tn, slab_out, tm=1024):
    S, D = x.shape
    N = w.shape[-1]
    if slab_out:
        slabs = tn // HEAD_DIM
        out_shape = jax.ShapeDtypeStruct((N // HEAD_DIM, S, HEAD_DIM), F32)
        out_spec = pl.BlockSpec((slabs, tm, HEAD_DIM), lambda i, j: (j, i, 0))
    else:
        slabs = 0
        out_shape = jax.ShapeDtypeStruct((S, N), F32)
        out_spec = pl.BlockSpec((tm, tn), lambda i, j: (i, j))
    return pl.pallas_call(
        functools.partial(_norm_proj_kernel, slabs=slabs),
        out_shape=out_shape,
        grid=(S // tm, N // tn),
        in_specs=[
            pl.BlockSpec((tm, D), lambda i, j: (i, 0)),
            pl.BlockSpec((None, 1, D), lambda i, j: (gain_idx, 0, 0)),
            pl.BlockSpec((None, D, tn), lambda i, j: (w_idx, 0, j)),
        ],
        out_specs=out_spec,
        scratch_shapes=[pltpu.VMEM((tm, D), BF16)],
        compiler_params=_params("parallel", "arbitrary"),
        name="norm_proj_slab" if slab_out else "norm_proj",
    )(x, gain, w)


def _proj_res_kernel(a_ref, w_ref, x_ref, o_ref):
    o_ref[...] = x_ref[...] + jnp.dot(a_ref[...], w_ref[...].astype(BF16), preferred_element_type=F32)


def _proj_res(a, w, w_idx, x, tm=512, tn=2048):
    S, D = x.shape
    K = a.shape[-1]
    return pl.pallas_call(
        _proj_res_kernel,
        out_shape=jax.ShapeDtypeStruct((S, D), F32),
        grid=(S // tm, D // tn),
        in_specs=[
            pl.BlockSpec((tm, K), lambda i, j: (i, 0)),
            pl.BlockSpec((None, K, tn), lambda i, j: (w_idx, 0, j)),
            pl.BlockSpec((tm, tn), lambda i, j: (i, j)),
        ],
        out_specs=pl.BlockSpec((tm, tn), lambda i, j: (i, j)),
        compiler_params=_params("parallel", "arbitrary"),
        name="proj_res",
    )(a, w, x)


LOG2E = float(np.log2(np.e))


def _dil_bias(slope, dilation, steps):
    valid = (steps >= 0) & (steps <= WINDOW_KEYS)
    return jnp.where(valid, (-slope * dilation * LOG2E) * steps.astype(F32), NEG)


def _gather_rows(ref, chunks):
    parts = [ref[pl.ds(start, size), :] for start, size in chunks]
    return parts[0] if len(parts) == 1 else jnp.concatenate(parts, axis=0)


def _dil_tiles(q_ref, k_ref, v_ref, acc_ref, m_ref, tiles, first):
    old = []
    for q_chunks, _, _ in tiles:
        old.append(None if first else (_gather_rows(m_ref, q_chunks), _gather_rows(acc_ref, q_chunks)))
    new = []
    for (q_chunks, k_chunks, bias), state in zip(tiles, old):
        q = (_gather_rows(q_ref, q_chunks) * (HEAD_DIM ** -0.5 * LOG2E)).astype(BF16)
        k = _gather_rows(k_ref, k_chunks).astype(BF16)
        v = _gather_rows(v_ref, k_chunks).astype(BF16)
        v1 = jnp.concatenate([v, jnp.ones_like(v)], axis=1)
        s = lax.dot_general(q, k, (((1,), (1,)), ((), ())), preferred_element_type=F32) + bias
        m_tile = jnp.max(s, axis=-1, keepdims=True)
        if first:
            m_new = jnp.broadcast_to(m_tile, (WINDOW_KEYS, HEAD_DIM))
            p = jnp.exp2(s - m_tile)
            acc_new = jnp.dot(p.astype(BF16), v1, preferred_element_type=F32)
        else:
            m_old, acc_old = state
            m_new = jnp.maximum(m_old, m_tile)
            alpha = jnp.exp2(m_old - m_new)
            reps = s.shape[1] // HEAD_DIM
            m_wide = m_new if reps == 1 else jnp.concatenate([m_new] * reps, axis=1)
            p = jnp.exp2(s - m_wide)
            acc_new = (jnp.concatenate([alpha, alpha], axis=1) * acc_old
                       + jnp.dot(p.astype(BF16), v1, preferred_element_type=F32))
        new.append((m_new, acc_new))
    for (q_chunks, _, _), (m_new, acc_new) in zip(tiles, new):
        off = 0
        for start, size in q_chunks:
            m_ref[pl.ds(start, size), :] = m_new[off:off + size]
            acc_ref[pl.ds(start, size), :] = acc_new[off:off + size]
            off += size


def _dilated_kernel(slopes_ref, q_ref, k_ref, v_ref, o_ref, acc_ref, m_ref, *, batch1, batch2, batch3):
    S = q_ref.shape[0]
    A = S // RESIDUES
    n = WINDOW_KEYS
    slope = slopes_ref[pl.program_id(0)]
    run = functools.partial(_dil_tiles, q_ref, k_ref, v_ref, acc_ref, m_ref)

    def iota(shape, dim):
        return lax.broadcasted_iota(jnp.int32, shape, dim)

    def al(x):
        return x if isinstance(x, int) else pl.multiple_of(x, 8)

    qi, kj = iota((n, 2 * n), 0), iota((n, 2 * n), 1)
    steps = n + 16 * ((qi & 7) - (kj & 15)) + ((qi >> 3) - (kj >> 4))
    bias1 = _dil_bias(slope, 1.0, steps)
    qi, kj = iota((n, n), 0), iota((n, n), 1)
    steps = 16 * ((qi & 7) - (kj & 7)) + ((qi >> 3) - (kj >> 3))
    bias1_first = _dil_bias(slope, 1.0, steps)

    def b1(c, has_prev):
        qc = [(al(r * A + c * 8), 8) for r in range(RESIDUES)]
        if has_prev:
            kc = [(al(r * A + (c - 1) * 8), 16) for r in range(RESIDUES)]
        else:
            kc = qc
        return qc, kc, bias1 if has_prev else bias1_first

    run([b1(0, False)], True)

    def b1_body(g, carry):
        run([b1(1 + g * batch1 + t, True) for t in range(batch1)], True)
        return carry

    lax.fori_loop(0, (S // n - 1) // batch1, b1_body, 0)

    qi, kj = iota((n, 2 * n), 0), iota((n, 2 * n), 1)
    steps = n + 4 * ((qi & 31) - (kj & 63)) + ((qi >> 5) - (kj >> 6))
    bias2 = _dil_bias(slope, 4.0, steps)
    qi, kj = iota((n, n), 0), iota((n, n), 1)
    steps = 4 * ((qi & 31) - (kj & 31)) + ((qi >> 5) - (kj >> 5))
    bias2_first = _dil_bias(slope, 4.0, steps)

    def b2(r4, c, has_prev):
        qc = [(al((r4 + 4 * s) * A + c * 32), 32) for s in range(4)]
        if has_prev:
            kc = [(al((r4 + 4 * s) * A + (c - 1) * 32), 64) for s in range(4)]
        else:
            kc = qc
        return qc, kc, bias2 if has_prev else bias2_first

    def b2_outer(r4, carry):
        def b2_body(g, carry2):
            run([b2(r4, 1 + g * batch2 + t, True) for t in range(batch2)], False)
            return carry2

        run([b2(r4, 0, False)], False)
        lax.fori_loop(0, (S // (4 * n) - 1) // batch2, b2_body, 0)
        return carry

    lax.fori_loop(0, 4, b2_outer, 0)

    qi, kj = iota((n, 2 * n), 0), iota((n, 2 * n), 1)
    bias3 = _dil_bias(slope, 16.0, n + qi - kj)
    qi, kj = iota((n, n), 0), iota((n, n), 1)
    bias3_first = _dil_bias(slope, 16.0, qi - kj)

    def b3_body(g, carry):
        tiles = []
        for t in range(batch3):
            base = (batch3 * g + t) * A
            tiles.append(([(al(base), n)], [(al(base), n)], bias3_first))
            for c in range(1, A // n):
                tiles.append(([(al(base + c * n), n)], [(al(base + (c - 1) * n), 2 * n)], bias3))
        run(tiles, False)
        return carry

    lax.fori_loop(0, RESIDUES // batch3, b3_body, 0)

    o_ref[...] = (acc_ref[:, :HEAD_DIM] / acc_ref[:, HEAD_DIM:]).astype(o_ref.dtype)


def _dilated_attention(qkv, slopes):
    S = qkv.shape[1]
    H = N_HEADS

    def slab(offset):
        return pl.BlockSpec((None, S, HEAD_DIM), lambda h: (offset + h, 0, 0))

    return pl.pallas_call(
        functools.partial(_dilated_kernel, batch1=9, batch2=15, batch3=4),
        out_shape=jax.ShapeDtypeStruct((S, H * HEAD_DIM), BF16),
        grid=(H,),
        in_specs=[pl.BlockSpec(memory_space=pltpu.SMEM), slab(0), slab(H), slab(2 * H)],
        out_specs=pl.BlockSpec((S, HEAD_DIM), lambda h: (0, h)),
        scratch_shapes=[pltpu.VMEM((S, 2 * HEAD_DIM), F32), pltpu.VMEM((S, HEAD_DIM), F32)],
        compiler_params=_params("parallel"),
        name="dilated_attn",
    )(slopes, qkv, qkv, qkv)


def _rope_pad(z, cos_ref, sin_ref):
    return z * cos_ref[...] + pltpu.roll(z, ROPE_DIM, axis=1) * sin_ref[...]


def _mla_q_kernel(t_ref, g_ref, w_ref, cos_ref, sin_ref, o_ref, cq_ref, *, heads, scale):
    @pl.when(pl.program_id(1) == 0)
    def _():
        cq_ref[...] = _rms(t_ref[...], g_ref[...]).astype(BF16)

    y = jnp.dot(cq_ref[...], w_ref[...], preferred_element_type=F32)
    for hh in range(heads):
        base = hh * QK_PAD
        o_ref[hh, :, :HEAD_DIM] = (y[:, base:base + HEAD_DIM] * scale).astype(BF16)
        z = y[:, base + HEAD_DIM:base + QK_PAD]
        o_ref[hh, :, HEAD_DIM:] = (_rope_pad(z, cos_ref, sin_ref) * scale).astype(BF16)


def _mla_q(t, gain, gain_idx, w, w_idx, cos_pad, sin_pad, scale, tm=1024, heads=8):
    S, C = t.shape
    return pl.pallas_call(
        functools.partial(_mla_q_kernel, heads=heads, scale=scale),
        out_shape=jax.ShapeDtypeStruct((N_HEADS, S, QK_PAD), BF16),
        grid=(S // tm, N_HEADS // heads),
        in_specs=[
            pl.BlockSpec((tm, C), lambda i, j: (i, 0)),
            pl.BlockSpec((None, 1, C), lambda i, j: (gain_idx, 0, 0)),
            pl.BlockSpec((None, C, heads * QK_PAD), lambda i, j: (w_idx, 0, j)),
            pl.BlockSpec((tm, HEAD_DIM), lambda i, j: (i, 0)),
            pl.BlockSpec((tm, HEAD_DIM), lambda i, j: (i, 0)),
        ],
        out_specs=pl.BlockSpec((heads, tm, QK_PAD), lambda i, j: (j, i, 0)),
        scratch_shapes=[pltpu.VMEM((tm, C), BF16)],
        compiler_params=_params("parallel", "arbitrary"),
        name="mla_q",
    )(t, gain, w, cos_pad, sin_pad)


def _mla_kv_kernel(t_ref, g_ref, wuk_ref, wuv_ref, cos_ref, sin_ref, k_ref, v_ref):
    ckv = _rms(t_ref[:, :KV_LORA], g_ref[...]).astype(BF16)
    k_nope = jnp.dot(ckv, wuk_ref[...], preferred_element_type=F32)
    v = jnp.dot(ckv, wuv_ref[...], preferred_element_type=F32)
    k_rope = _rope_pad(t_ref[:, KV_LORA:], cos_ref, sin_ref).astype(BF16)
    ones = jnp.ones((t_ref.shape[0], HEAD_DIM), BF16)
    for h in range(N_HEADS):
        cols = slice(h * HEAD_DIM, (h + 1) * HEAD_DIM)
        k_ref[h, :, :HEAD_DIM] = k_nope[:, cols].astype(BF16)
        k_ref[h, :, HEAD_DIM:] = k_rope
        v_ref[h, :, :HEAD_DIM] = v[:, cols].astype(BF16)
        v_ref[h, :, HEAD_DIM:] = ones


def _mla_kv(t, gain, wuk, wuv, cos_pad, sin_pad, tm=512):
    S, W = t.shape
    HD = N_HEADS * HEAD_DIM
    return pl.pallas_call(
        _mla_kv_kernel,
        out_shape=(jax.ShapeDtypeStruct((N_HEADS, S, QK_PAD), BF16),
                   jax.ShapeDtypeStruct((N_HEADS, S, 2 * HEAD_DIM), BF16)),
        grid=(S // tm,),
        in_specs=[
            pl.BlockSpec((tm, W), lambda i: (i, 0)),
            pl.BlockSpec((1, KV_LORA), lambda i: (0, 0)),
            pl.BlockSpec((KV_LORA, HD), lambda i: (0, 0)),
            pl.BlockSpec((KV_LORA, HD), lambda i: (0, 0)),
            pl.BlockSpec((tm, HEAD_DIM), lambda i: (i, 0)),
            pl.BlockSpec((tm, HEAD_DIM), lambda i: (i, 0)),
        ],
        out_specs=(pl.BlockSpec((N_HEADS, tm, QK_PAD), lambda i: (0, i, 0)),
                   pl.BlockSpec((N_HEADS, tm, 2 * HEAD_DIM), lambda i: (0, i, 0))),
        compiler_params=_params("parallel"),
        name="mla_kv",
    )(t, gain, wuk, wuv, cos_pad, sin_pad)


def _mla_attn_kernel(q_ref, k_ref, v_ref, o_ref, s_ref, bmax_ref, m_ref, acc_ref, *, blk, chains):
    i = pl.program_id(1)
    rows = blk // chains
    nt = (((1,), (1,)), ((), ()))

    def scores(j, slot, masked):
        k = k_ref[pl.ds(pl.multiple_of(j * blk, blk), blk), :]
        for c in range(chains):
            r = slice(c * rows, (c + 1) * rows)
            s = lax.dot_general(q_ref[r, :], k, nt, preferred_element_type=F32)
            if masked:
                row = lax.broadcasted_iota(jnp.int32, s.shape, 0) + c * rows
                col = lax.broadcasted_iota(jnp.int32, s.shape, 1)
                s = jnp.where(col <= row, s, NEG)
            s_ref[slot, r, :] = s
            bmax_ref[slot, r, :] = jnp.broadcast_to(jnp.max(s, axis=-1, keepdims=True), (rows, HEAD_DIM))

    def update(j, slot):
        v = v_ref[pl.ds(pl.multiple_of(j * blk, blk), blk), :]
        for c in range(chains):
            r = slice(c * rows, (c + 1) * rows)
            m_old = m_ref[r, :]
            m_new = jnp.maximum(m_old, bmax_ref[slot, r, :])
            alpha = jnp.exp2(m_old - m_new)
            p = jnp.exp2(s_ref[slot, r, :] - jnp.concatenate([m_new] * (blk // HEAD_DIM), axis=1))
            acc_ref[r, :] = (jnp.concatenate([alpha, alpha], axis=1) * acc_ref[r, :]
                             + jnp.dot(p.astype(BF16), v, preferred_element_type=F32))
            m_ref[r, :] = m_new

    m_ref[...] = jnp.full(m_ref.shape, NEG, F32)
    acc_ref[...] = jnp.zeros(acc_ref.shape, F32)

    @pl.when(i == 0)
    def _():
        scores(0, 0, True)

    @pl.when(i > 0)
    def _():
        scores(0, 0, False)

    def pair(jj, carry):
        j = 2 * jj
        scores(j + 1, 1, False)
        update(j, 0)
        scores(j + 2, 0, False)
        update(j + 1, 1)
        return carry

    lax.fori_loop(0, (i - 1) // 2, pair, 0)

    @pl.when((i > 0) & (i % 2 == 0))
    def _():
        scores(i - 1, 1, False)
        update(i - 2, 0)
        scores(i, 0, True)
        update(i - 1, 1)
        update(i, 0)

    @pl.when(i % 2 == 1)
    def _():
        scores(i, 1, True)
        update(i - 1, 0)
        update(i, 1)

    @pl.when(i == 0)
    def _():
        update(0, 0)

    o_ref[...] = (acc_ref[:, :HEAD_DIM] / acc_ref[:, HEAD_DIM:]).astype(o_ref.dtype)


def _mla_attention(q, k, v, blk=1024, chains=1):
    H, S, _ = q.shape
    return pl.pallas_call(
        functools.partial(_mla_attn_kernel, blk=blk, chains=chains),
        out_shape=jax.ShapeDtypeStruct((S, H * HEAD_DIM), BF16),
        grid=(H, S // blk),
        in_specs=[
            pl.BlockSpec((None, blk, QK_PAD), lambda h, i: (h, i, 0)),
            pl.BlockSpec((None, S, QK_PAD), lambda h, i: (h, 0, 0)),
            pl.BlockSpec((None, S, 2 * HEAD_DIM), lambda h, i: (h, 0, 0)),
        ],
        out_specs=pl.BlockSpec((blk, HEAD_DIM), lambda h, i: (i, h)),
        scratch_shapes=[
            pltpu.VMEM((2, blk, blk), F32),
            pltpu.VMEM((2, blk, HEAD_DIM), F32),
            pltpu.VMEM((blk, HEAD_DIM), F32),
            pltpu.VMEM((blk, 2 * HEAD_DIM), F32),
        ],
        compiler_params=_params("parallel", "arbitrary"),
        name="mla_attn",
    )(q, k, v)


def _rope_tables(seq):
    inv = 1.0 / (ROPE_THETA ** (jnp.arange(0, ROPE_DIM, 2, dtype=F32) / ROPE_DIM))
    ang = jnp.arange(seq, dtype=F32)[:, None] * inv[None, :]
    cos, sin = jnp.cos(ang), jnp.sin(ang)
    zeros = jnp.zeros((seq, HEAD_DIM - ROPE_DIM), F32)
    cos_pad = jnp.concatenate([cos, cos, zeros], axis=1)
    sin_pad = jnp.concatenate([-sin, sin, zeros], axis=1)
    return cos_pad, sin_pad


def kernel(x, ffn_norm1, ffn1_wg, ffn1_wu, ffn1_wd, mix_norm, ffn_norm2, ffn2_wg, ffn2_wu, ffn2_wd,
           a_wqkv, a_wo, kv_norm, b_wdkv, b_ckv_norm, b_wkr, b_wuk, b_wuv,
           b_wdq, b_cq_norm, b_wuq, b_wo, final_norm):
    B, S, D = x.shape
    assert (B, S, D) == (1, SEQ, D_MODEL)
    bf = lambda w: w.astype(BF16)
    gains = lambda g: g.reshape(g.shape[0], 1, g.shape[-1])

    w1g, w1u, w1d = ffn1_wg, ffn1_wu, ffn1_wd
    w2g, w2u, w2d = ffn2_wg, ffn2_wu, ffn2_wd
    wqkv, wo_a, wo_b = a_wqkv, a_wo, b_wo
    g1, g2, gm = gains(ffn_norm1), gains(ffn_norm2), gains(mix_norm)

    rot = (np.arange(ROPE_DIM) + ROPE_DIM // 2) % ROPE_DIM
    w_kv_down = bf(jnp.concatenate([b_wdkv, b_wkr, b_wkr[:, rot]], axis=1))[None]
    wuk = bf(b_wuk.reshape(KV_LORA, N_HEADS * HEAD_DIM))
    wuv = bf(b_wuv.reshape(KV_LORA, N_HEADS * HEAD_DIM))
    wdq = bf(b_wdq)
    q_rope_w = b_wuq[..., HEAD_DIM:]
    wuq = bf(jnp.concatenate([b_wuq, q_rope_w[..., rot]], axis=-1).reshape(
        b_wuq.shape[0], Q_LORA, N_HEADS * QK_PAD))

    slopes = jnp.asarray(2.0 ** (-8.0 * (np.arange(N_HEADS) + 1) / N_HEADS), dtype=F32)
    cos_pad, sin_pad = _rope_tables(S)
    scale = float((HEAD_DIM + ROPE_DIM) ** -0.5 * np.log2(np.e))

    xs = x[0]
    xs = xs.reshape(S // RESIDUES, RESIDUES, D).transpose(1, 0, 2).reshape(S, D)
    for layer in range(N_A_LAYERS):
        xs = _ffn(xs, g1, w1g, w1u, w1d, layer)
        qkv = _norm_proj(xs, gm, layer, wqkv, layer, tn=512, slab_out=True)
        o = _dilated_attention(qkv, slopes)
        xs = _proj_res(o, wo_a, layer, xs)
        xs = _ffn(xs, g2, w2g, w2u, w2d, layer)
    xs = xs.reshape(RESIDUES, S // RESIDUES, D).transpose(1, 0, 2).reshape(S, D)

    t_kv = _norm_proj(xs, kv_norm.reshape(1, 1, D), 0, w_kv_down, 0, tn=w_kv_down.shape[-1], slab_out=False)
    k_all, v_all = _mla_kv(t_kv, b_ckv_norm.reshape(1, KV_LORA), wuk, wuv, cos_pad, sin_pad)
    gq = gains(b_cq_norm)
    for layer in range(N_A_LAYERS, DEPTH):
        jb = layer - N_A_LAYERS
        xs = _ffn(xs, g1, w1g, w1u, w1d, layer)
        t_q = _norm_proj(xs, gm, layer, wdq, jb, tn=Q_LORA, slab_out=False)
        q_all = _mla_q(t_q, gq, jb, wuq, jb, cos_pad, sin_pad, scale)
        o = _mla_attention(q_all, k_all, v_all)
        xs = _proj_res(o, wo_b, jb, xs)
        last = layer == DEPTH - 1
        xs = _ffn(xs, g2, w2g, w2u, w2d, layer, final_gain=final_norm.reshape(1, D) if last else None)
    return xs[None]
```

```python
import functools

import numpy as np
import jax
import jax.numpy as jnp
from jax import lax
from jax.experimental import pallas as pl
from jax.experimental.pallas import tpu as pltpu

F32 = jnp.float32
BF16 = jnp.bfloat16

D_MODEL = 2048
SEQ = 8192
DEPTH = 4
N_A_LAYERS = DEPTH // 2
EPS = 1e-6
D_FF = 5632
HEAD_DIM = 128
N_HEADS = 16
KV_LORA = 512
Q_LORA = 512
ROPE_DIM = 64
ROPE_THETA = 10000.0
QK_PAD = 256
WINDOW_KEYS = 128
RESIDUES = 16
NEG = -1e30

VMEM_LIMIT_BYTES = 56 * 1024 * 1024


def _params(*semantics):
    return pltpu.CompilerParams(dimension_semantics=semantics, vmem_limit_bytes=VMEM_LIMIT_BYTES)


def _rms(x, g):
    return x * lax.rsqrt(jnp.mean(x * x, axis=-1, keepdims=True) + EPS) * g


def _ffn_kernel(x_ref, g_ref, wg_ref, wu_ref, wd_ref, *rest, final):
    if final:
        fn_ref, o_ref, h_ref = rest
    else:
        o_ref, h_ref = rest
    f = pl.program_id(1)

    @pl.when(f == 0)
    def _():
        x = x_ref[...]
        h_ref[...] = _rms(x, g_ref[...]).astype(BF16)
        o_ref[...] = x

    h = h_ref[...]
    gate = jnp.dot(h, wg_ref[...].astype(BF16), preferred_element_type=F32)
    up = jnp.dot(h, wu_ref[...].astype(BF16), preferred_element_type=F32)
    act = (0.5 * (gate * jax.nn.sigmoid(gate)) * up).astype(BF16)
    o_ref[...] += jnp.dot(act, wd_ref[...].astype(BF16), preferred_element_type=F32)

    if final:
        @pl.when(f == pl.num_programs(1) - 1)
        def _():
            o_ref[...] = _rms(o_ref[...], fn_ref[...])


def _ffn(x, gain, wg, wu, wd, layer, final_gain=None, tm=1024, tf=256):
    S, D = x.shape
    F = wg.shape[-1]
    final = final_gain is not None
    in_specs = [
        pl.BlockSpec((tm, D), lambda i, f: (i, 0)),
        pl.BlockSpec((None, 1, D), lambda i, f: (layer, 0, 0)),
        pl.BlockSpec((None, D, tf), lambda i, f: (layer, 0, f)),
        pl.BlockSpec((None, D, tf), lambda i, f: (layer, 0, f)),
        pl.BlockSpec((None, tf, D), lambda i, f: (layer, f, 0)),
    ]
    args = [x, gain, wg, wu, wd]
    if final:
        in_specs.append(pl.BlockSpec((1, D), lambda i, f: (0, 0)))
        args.append(final_gain)
    return pl.pallas_call(
        functools.partial(_ffn_kernel, final=final),
        out_shape=jax.ShapeDtypeStruct((S, D), F32),
        grid=(S // tm, F // tf),
        in_specs=in_specs,
        out_specs=pl.BlockSpec((tm, D), lambda i, f: (i, 0)),
        scratch_shapes=[pltpu.VMEM((tm, D), BF16)],
        compiler_params=_params("parallel", "arbitrary"),
        name="ffn_final" if final else "ffn",
    )(*args)


def _norm_proj_kernel(x_ref, g_ref, w_ref, o_ref, h_ref, *, slabs):
    @pl.when(pl.program_id(1) == 0)
    def _():
        h_ref[...] = _rms(x_ref[...], g_ref[...]).astype(BF16)

    y = jnp.dot(h_ref[...], w_ref[...].astype(BF16), preferred_element_type=F32)
    if slabs:
        for s in range(slabs):
            o_ref[s] = y[:, s * HEAD_DIM:(s + 1) * HEAD_DIM]
    else:
        o_ref[...] = y


def _norm_proj(x, gain, gain_idx, w, w_idx, tn, slab_out, tm=1024):
    S, D = x.shape
    N = w.shape[-1]
    if slab_out:
        slabs = tn // HEAD_DIM
        out_shape = jax.ShapeDtypeStruct((N // HEAD_DIM, S, HEAD_DIM), F32)
        out_spec = pl.BlockSpec((slabs, tm, HEAD_DIM), lambda i, j: (j, i, 0))
    else:
        slabs = 0
        out_shape = jax.ShapeDtypeStruct((S, N), F32)
        out_spec = pl.BlockSpec((tm, tn), lambda i, j: (i, j))
    return pl.pallas_call(
        functools.partial(_norm_proj_kernel, slabs=slabs),
        out_shape=out_shape,
        grid=(S // tm, N // tn),
        in_specs=[
            pl.BlockSpec((tm, D), lambda i, j: (i, 0)),
            pl.BlockSpec((None, 1, D), lambda i, j: (gain_idx, 0, 0)),
            pl.BlockSpec((None, D, tn), lambda i, j: (w_idx, 0, j)),
        ],
        out_specs=out_spec,
        scratch_shapes=[pltpu.VMEM((tm, D), BF16)],
        compiler_params=_params("parallel", "arbitrary"),
        name="norm_proj_slab" if slab_out else "norm_proj",
    )(x, gain, w)


def _proj_res_kernel(a_ref, w_ref, x_ref, o_ref):
    o_ref[...] = x_ref[...] + jnp.dot(a_ref[...], w_ref[...].astype(BF16), preferred_element_type=F32)


def _proj_res(a, w, w_idx, x, tm=512, tn=2048):
    S, D = x.shape
    K = a.shape[-1]
    return pl.pallas_call(
        _proj_res_kernel,
        out_shape=jax.ShapeDtypeStruct((S, D), F32),
        grid=(S // tm, D // tn),
        in_specs=[
            pl.BlockSpec((tm, K), lambda i, j: (i, 0)),
            pl.BlockSpec((None, K, tn), lambda i, j: (w_idx, 0, j)),
            pl.BlockSpec((tm, tn), lambda i, j: (i, j)),
        ],
        out_specs=pl.BlockSpec((tm, tn), lambda i, j: (i, j)),
        compiler_params=_params("parallel", "arbitrary"),
        name="proj_res",
    )(a, w, x)


LOG2E = float(np.log2(np.e))


def _dil_bias(slope, dilation, steps):
    valid = (steps >= 0) & (steps <= WINDOW_KEYS)
    return jnp.where(valid, (-slope * dilation * LOG2E) * steps.astype(F32), NEG)


def _gather_rows(ref, chunks):
    parts = [ref[pl.ds(start, size), :] for start, size in chunks]
    return parts[0] if len(parts) == 1 else jnp.concatenate(parts, axis=0)


def _dil_tiles(q_ref, k_ref, v_ref, acc_ref, m_ref, tiles, first):
    old = []
    for q_chunks, _, _ in tiles:
        old.append(None if first else (_gather_rows(m_ref, q_chunks), _gather_rows(acc_ref, q_chunks)))
    new = []
    for (q_chunks, k_chunks, bias), state in zip(tiles, old):
        q = (_gather_rows(q_ref, q_chunks) * (HEAD_DIM ** -0.5 * LOG2E)).astype(BF16)
        k = _gather_rows(k_ref, k_chunks).astype(BF16)
        v = _gather_rows(v_ref, k_chunks).astype(BF16)
        v1 = jnp.concatenate([v, jnp.ones_like(v)], axis=1)
        s = lax.dot_general(q, k, (((1,), (1,)), ((), ())), preferred_element_type=F32) + bias
        m_tile = jnp.max(s, axis=-1, keepdims=True)
        if first:
            m_new = jnp.broadcast_to(m_tile, (WINDOW_KEYS, HEAD_DIM))
            p = jnp.exp2(s - m_tile)
            acc_new = jnp.dot(p.astype(BF16), v1, preferred_element_type=F32)
        else:
            m_old, acc_old = state
            m_new = jnp.maximum(m_old, m_tile)
            alpha = jnp.exp2(m_old - m_new)
            reps = s.shape[1] // HEAD_DIM
            m_wide = m_new if reps == 1 else jnp.concatenate([m_new] * reps, axis=1)
            p = jnp.exp2(s - m_wide)
            acc_new = (jnp.concatenate([alpha, alpha], axis=1) * acc_old
                       + jnp.dot(p.astype(BF16), v1, preferred_element_type=F32))
        new.append((m_new, acc_new))
    for (q_chunks, _, _), (m_new, acc_new) in zip(tiles, new):
        off = 0
        for start, size in q_chunks:
            m_ref[pl.ds(start, size), :] = m_new[off:off + size]
            acc_ref[pl.ds(start, size), :] = acc_new[off:off + size]
            off += size


def _dilated_kernel(slopes_ref, q_ref, k_ref, v_ref, o_ref, acc_ref, m_ref, *, batch1, batch2, batch3):
    S = q_ref.shape[0]
    A = S // RESIDUES
    n = WINDOW_KEYS
    slope = slopes_ref[pl.program_id(0)]
    run = functools.partial(_dil_tiles, q_ref, k_ref, v_ref, acc_ref, m_ref)

    def iota(shape, dim):
        return lax.broadcasted_iota(jnp.int32, shape, dim)

    def al(x):
        return x if isinstance(x, int) else pl.multiple_of(x, 8)

    qi, kj = iota((n, 2 * n), 0), iota((n, 2 * n), 1)
    steps = n + 16 * ((qi & 7) - (kj & 15)) + ((qi >> 3) - (kj >> 4))
    bias1 = _dil_bias(slope, 1.0, steps)
    qi, kj = iota((n, n), 0), iota((n, n), 1)
    steps = 16 * ((qi & 7) - (kj & 7)) + ((qi >> 3) - (kj >> 3))
    bias1_first = _dil_bias(slope, 1.0, steps)

    def b1(c, has_prev):
        qc = [(al(r * A + c * 8), 8) for r in range(RESIDUES)]
        if has_prev:
            kc = [(al(r * A + (c - 1) * 8), 16) for r in range(RESIDUES)]
        else:
            kc = qc
        return qc, kc, bias1 if has_prev else bias1_first

    run([b1(0, False)], True)

    def b1_body(g, carry):
        run([b1(1 + g * batch1 + t, True) for t in range(batch1)], True)
        return carry

    lax.fori_loop(0, (S // n - 1) // batch1, b1_body, 0)

    qi, kj = iota((n, 2 * n), 0), iota((n, 2 * n), 1)
    steps = n + 4 * ((qi & 31) - (kj & 63)) + ((qi >> 5) - (kj >> 6))
    bias2 = _dil_bias(slope, 4.0, steps)
    qi, kj = iota((n, n), 0), iota((n, n), 1)
    steps = 4 * ((qi & 31) - (kj & 31)) + ((qi >> 5) - (kj >> 5))
    bias2_first = _dil_bias(slope, 4.0, steps)

    def b2(r4, c, has_prev):
        qc = [(al((r4 + 4 * s) * A + c * 32), 32) for s in range(4)]
        if has_prev:
            kc = [(al((r4 + 4 * s) * A + (c - 1) * 32), 64) for s in range(4)]
        else:
            kc = qc
        return qc, kc, bias2 if has_prev else bias2_first

    def b2_outer(r4, carry):
        def b2_body(g, carry2):
            run([b2(r4, 1 + g * batch2 + t, True) for t in range(batch2)], False)
            return carry2

        run([b2(r4, 0, False)], False)
        lax.fori_loop(0, (S // (4 * n) - 1) // batch2, b2_body, 0)
        return carry

    lax.fori_loop(0, 4, b2_outer, 0)

    qi, kj = iota((n, 2 * n), 0), iota((n, 2 * n), 1)
    bias3 = _dil_bias(slope, 16.0, n + qi - kj)
    qi, kj = iota((n, n), 0), iota((n, n), 1)
    bias3_first = _dil_bias(slope, 16.0, qi - kj)

    def b3_body(g, carry):
        tiles = []
        for t in range(batch3):
            base = (batch3 * g + t) * A
            tiles.append(([(al(base), n)], [(al(base), n)], bias3_first))
            for c in range(1, A // n):
                tiles.append(([(al(base + c * n), n)], [(al(base + (c - 1) * n), 2 * n)], bias3))
        run(tiles, False)
        return carry

    lax.fori_loop(0, RESIDUES // batch3, b3_body, 0)

    o_ref[...] = (acc_ref[:, :HEAD_DIM] / acc_ref[:, HEAD_DIM:]).astype(o_ref.dtype)


def _dilated_attention(qkv, slopes):
    S = qkv.shape[1]
    H = N_HEADS

    def slab(offset):
        return pl.BlockSpec((None, S, HEAD_DIM), lambda h: (offset + h, 0, 0))

    return pl.pallas_call(
        functools.partial(_dilated_kernel, batch1=9, batch2=15, batch3=4),
        out_shape=jax.ShapeDtypeStruct((S, H * HEAD_DIM), BF16),
        grid=(H,),
        in_specs=[pl.BlockSpec(memory_space=pltpu.SMEM), slab(0), slab(H), slab(2 * H)],
        out_specs=pl.BlockSpec((S, HEAD_DIM), lambda h: (0, h)),
        scratch_shapes=[pltpu.VMEM((S, 2 * HEAD_DIM), F32), pltpu.VMEM((S, HEAD_DIM), F32)],
        compiler_params=_params("parallel"),
        name="dilated_attn",
    )(slopes, qkv, qkv, qkv)


def _rope_pad(z, cos_ref, sin_ref):
    return z * cos_ref[...] + pltpu.roll(z, ROPE_DIM, axis=1) * sin_ref[...]


def _mla_q_kernel(t_ref, g_ref, w_ref, cos_ref, sin_ref, o_ref, cq_ref, *, heads, scale):
    @pl.when(pl.program_id(1) == 0)
    def _():
        cq_ref[...] = _rms(t_ref[...], g_ref[...]).astype(BF16)

    y = jnp.dot(cq_ref[...], w_ref[...], preferred_element_type=F32)
    for hh in range(heads):
        base = hh * QK_PAD
        o_ref[hh, :, :HEAD_DIM] = (y[:, base:base + HEAD_DIM] * scale).astype(BF16)
        z = y[:, base + HEAD_DIM:base + QK_PAD]
        o_ref[hh, :, HEAD_DIM:] = (_rope_pad(z, cos_ref, sin_ref) * scale).astype(BF16)


def _mla_q(t, gain, gain_idx, w, w_idx, cos_pad, sin_pad, scale, tm=1024, heads=8):
    S, C = t.shape
    return pl.pallas_call(
        functools.partial(_mla_q_kernel, heads=heads, scale=scale),
        out_shape=jax.ShapeDtypeStruct((N_HEADS, S, QK_PAD), BF16),
        grid=(S // tm, N_HEADS // heads),
        in_specs=[
            pl.BlockSpec((tm, C), lambda i, j: (i, 0)),
            pl.BlockSpec((None, 1, C), lambda i, j: (gain_idx, 0, 0)),
            pl.BlockSpec((None, C, heads * QK_PAD), lambda i, j: (w_idx, 0, j)),
            pl.BlockSpec((tm, HEAD_DIM), lambda i, j: (i, 0)),
            pl.BlockSpec((tm, HEAD_DIM), lambda i, j: (i, 0)),
        ],
        out_specs=pl.BlockSpec((heads, tm, QK_PAD), lambda i, j: (j, i, 0)),
        scratch_shapes=[pltpu.VMEM((tm, C), BF16)],
        compiler_params=_params("parallel", "arbitrary"),
        name="mla_q",
    )(t, gain, w, cos_pad, sin_pad)


def _mla_kv_kernel(t_ref, g_ref, wuk_ref, wuv_ref, cos_ref, sin_ref, k_ref, v_ref):
    ckv = _rms(t_ref[:, :KV_LORA], g_ref[...]).astype(BF16)
    k_nope = jnp.dot(ckv, wuk_ref[...], preferred_element_type=F32)
    v = jnp.dot(ckv, wuv_ref[...], preferred_element_type=F32)
    k_rope = _rope_pad(t_ref[:, KV_LORA:], cos_ref, sin_ref).astype(BF16)
    ones = jnp.ones((t_ref.shape[0], HEAD_DIM), BF16)
    for h in range(N_HEADS):
        cols = slice(h * HEAD_DIM, (h + 1) * HEAD_DIM)
        k_ref[h, :, :HEAD_DIM] = k_nope[:, cols].astype(BF16)
        k_ref[h, :, HEAD_DIM:] = k_rope
        v_ref[h, :, :HEAD_DIM] = v[:, cols].astype(BF16)
        v_ref[h, :, HEAD_DIM:] = ones


def _mla_kv(t, gain, wuk, wuv, cos_pad, sin_pad, tm=512):
    S, W = t.shape
    HD = N_HEADS * HEAD_DIM
    return pl.pallas_call(
        _mla_kv_kernel,
        out_shape=(jax.ShapeDtypeStruct((N_HEADS, S, QK_PAD), BF16),
                   jax.ShapeDtypeStruct((N_HEADS, S, 2 * HEAD_DIM), BF16)),
        grid=(S // tm,),
        in_specs=[
            pl.BlockSpec((tm, W), lambda i: (i, 0)),
            pl.BlockSpec((1, KV_LORA), lambda i: (0, 0)),
            pl.BlockSpec((KV_LORA, HD), lambda i: (0, 0)),
            pl.BlockSpec((KV_LORA, HD), lambda i: (0, 0)),
            pl.BlockSpec((tm, HEAD_DIM), lambda i: (i, 0)),
            pl.BlockSpec((tm, HEAD_DIM), lambda i: (i, 0)),
        ],
        out_specs=(pl.BlockSpec((N_HEADS, tm, QK_PAD), lambda i: (0, i, 0)),
                   pl.BlockSpec((N_HEADS, tm, 2 * HEAD_DIM), lambda i: (0, i, 0))),
        compiler_params=_params("parallel"),
        name="mla_kv",
    )(t, gain, wuk, wuv, cos_pad, sin_pad)


def _mla_attn_kernel(q_ref, k_ref, v_ref, o_ref, s_ref, bmax_ref, m_ref, acc_ref, *, blk, chains):
    i = pl.program_id(1)
    rows = blk // chains
    nt = (((1,), (1,)), ((), ()))

    def scores(j, slot, masked):
        k = k_ref[pl.ds(pl.multiple_of(j * blk, blk), blk), :]
        for c in range(chains):
            r = slice(c * rows, (c + 1) * rows)
            s = lax.dot_general(q_ref[r, :], k, nt, preferred_element_type=F32)
            if masked:
                row = lax.broadcasted_iota(jnp.int32, s.shape, 0) + c * rows
                col = lax.broadcasted_iota(jnp.int32, s.shape, 1)
                s = jnp.where(col <= row, s, NEG)
            s_ref[slot, r, :] = s
            bmax_ref[slot, r, :] = jnp.broadcast_to(jnp.max(s, axis=-1, keepdims=True), (rows, HEAD_DIM))

    def update(j, slot):
        v = v_ref[pl.ds(pl.multiple_of(j * blk, blk), blk), :]
        for c in range(chains):
            r = slice(c * rows, (c + 1) * rows)
            m_old = m_ref[r, :]
            m_new = jnp.maximum(m_old, bmax_ref[slot, r, :])
            alpha = jnp.exp2(m_old - m_new)
            p = jnp.exp2(s_ref[slot, r, :] - jnp.concatenate([m_new] * (blk // HEAD_DIM), axis=1))
            acc_ref[r, :] = (jnp.concatenate([alpha, alpha], axis=1) * acc_ref[r, :]
                             + jnp.dot(p.astype(BF16), v, preferred_element_type=F32))
            m_ref[r, :] = m_new

    m_ref[...] = jnp.full(m_ref.shape, NEG, F32)
    acc_ref[...] = jnp.zeros(acc_ref.shape, F32)

    @pl.when(i == 0)
    def _():
        scores(0, 0, True)

    @pl.when(i > 0)
    def _():
        scores(0, 0, False)

    def pair(jj, carry):
        j = 2 * jj
        scores(j + 1, 1, False)
        update(j, 0)
        scores(j + 2, 0, False)
        update(j + 1, 1)
        return carry

    lax.fori_loop(0, (i - 1) // 2, pair, 0)

    @pl.when((i > 0) & (i % 2 == 0))
    def _():
        scores(i - 1, 1, False)
        update(i - 2, 0)
        scores(i, 0, True)
        update(i - 1, 1)
        update(i, 0)

    @pl.when(i % 2 == 1)
    def _():
        scores(i, 1, True)
        update(i - 1, 0)
        update(i, 1)

    @pl.when(i == 0)
    def _():
        update(0, 0)

    o_ref[...] = (acc_ref[:, :HEAD_DIM] / acc_ref[:, HEAD_DIM:]).astype(o_ref.dtype)


def _mla_attention(q, k, v, blk=1024, chains=1):
    H, S, _ = q.shape
    return pl.pallas_call(
        functools.partial(_mla_attn_kernel, blk=blk, chains=chains),
        out_shape=jax.ShapeDtypeStruct((S, H * HEAD_DIM), BF16),
        grid=(H, S // blk),
        in_specs=[
            pl.BlockSpec((None, blk, QK_PAD), lambda h, i: (h, i, 0)),
            pl.BlockSpec((None, S, QK_PAD), lambda h, i: (h, 0, 0)),
            pl.BlockSpec((None, S, 2 * HEAD_DIM), lambda h, i: (h, 0, 0)),
        ],
        out_specs=pl.BlockSpec((blk, HEAD_DIM), lambda h, i: (i, h)),
        scratch_shapes=[
            pltpu.VMEM((2, blk, blk), F32),
            pltpu.VMEM((2, blk, HEAD_DIM), F32),
            pltpu.VMEM((blk, HEAD_DIM), F32),
            pltpu.VMEM((blk, 2 * HEAD_DIM), F32),
        ],
        compiler_params=_params("parallel", "arbitrary"),
        name="mla_attn",
    )(q, k, v)


def _rope_tables(seq):
    inv = 1.0 / (ROPE_THETA ** (jnp.arange(0, ROPE_DIM, 2, dtype=F32) / ROPE_DIM))
    ang = jnp.arange(seq, dtype=F32)[:, None] * inv[None, :]
    cos, sin = jnp.cos(ang), jnp.sin(ang)
    zeros = jnp.zeros((seq, HEAD_DIM - ROPE_DIM), F32)
    cos_pad = jnp.concatenate([cos, cos, zeros], axis=1)
    sin_pad = jnp.concatenate([-sin, sin, zeros], axis=1)
    return cos_pad, sin_pad


def kernel(x, ffn_norm1, ffn1_wg, ffn1_wu, ffn1_wd, mix_norm, ffn_norm2, ffn2_wg, ffn2_wu, ffn2_wd,
           a_wqkv, a_wo, kv_norm, b_wdkv, b_ckv_norm, b_wkr, b_wuk, b_wuv,
           b_wdq, b_cq_norm, b_wuq, b_wo, final_norm):
    B, S, D = x.shape
    assert (B, S, D) == (1, SEQ, D_MODEL)
    bf = lambda w: w.astype(BF16)
    gains = lambda g: g.reshape(g.shape[0], 1, g.shape[-1])

    w1g, w1u, w1d = ffn1_wg, ffn1_wu, ffn1_wd
    w2g, w2u, w2d = ffn2_wg, ffn2_wu, ffn2_wd
    wqkv, wo_a, wo_b = a_wqkv, a_wo, b_wo
    g1, g2, gm = gains(ffn_norm1), gains(ffn_norm2), gains(mix_norm)

    rot = (np.arange(ROPE_DIM) + ROPE_DIM // 2) % ROPE_DIM
    w_kv_down = bf(jnp.concatenate([b_wdkv, b_wkr, b_wkr[:, rot]], axis=1))[None]
    wuk = bf(b_wuk.reshape(KV_LORA, N_HEADS * HEAD_DIM))
    wuv = bf(b_wuv.reshape(KV_LORA, N_HEADS * HEAD_DIM))
    wdq = bf(b_wdq)
    q_rope_w = b_wuq[..., HEAD_DIM:]
    wuq = bf(jnp.concatenate([b_wuq, q_rope_w[..., rot]], axis=-1).reshape(
        b_wuq.shape[0], Q_LORA, N_HEADS * QK_PAD))

    slopes = jnp.asarray(2.0 ** (-8.0 * (np.arange(N_HEADS) + 1) / N_HEADS), dtype=F32)
    cos_pad, sin_pad = _rope_tables(S)
    scale = float((HEAD_DIM + ROPE_DIM) ** -0.5 * np.log2(np.e))

    xs = x[0]
    xs = xs.reshape(S // RESIDUES, RESIDUES, D).transpose(1, 0, 2).reshape(S, D)
    for layer in range(N_A_LAYERS):
        xs = _ffn(xs, g1, w1g, w1u, w1d, layer)
        qkv = _norm_proj(xs, gm, layer, wqkv, layer, tn=256, slab_out=True, tm=2048)
        o = _dilated_attention(qkv, slopes)
        xs = _proj_res(o, wo_a, layer, xs)
        xs = _ffn(xs, g2, w2g, w2u, w2d, layer)
    xs = xs.reshape(RESIDUES, S // RESIDUES, D).transpose(1, 0, 2).reshape(S, D)

    t_kv = _norm_proj(xs, kv_norm.reshape(1, 1, D), 0, w_kv_down, 0, tn=w_kv_down.shape[-1], slab_out=False)
    k_all, v_all = _mla_kv(t_kv, b_ckv_norm.reshape(1, KV_LORA), wuk, wuv, cos_pad, sin_pad)
    gq = gains(b_cq_norm)
    for layer in range(N_A_LAYERS, DEPTH):
        jb = layer - N_A_LAYERS
        xs = _ffn(xs, g1, w1g, w1u, w1d, layer)
        t_q = _norm_proj(xs, gm, layer, wdq, jb, tn=Q_LORA, slab_out=False)
        q_all = _mla_q(t_q, gq, jb, wuq, jb, cos_pad, sin_pad, scale)
        o = _mla_attention(q_all, k_all, v_all)
        xs = _proj_res(o, wo_b, jb, xs)
        last = layer == DEPTH - 1
        xs = _ffn(xs, g2, w2g, w2u, w2d, layer, final_gain=final_norm.reshape(1, D) if last else None)
    return xs[None]
```

```python
import functools

import numpy as np
import jax
import jax.numpy as jnp
from jax import lax
from jax.experimental import pallas as pl
from jax.experimental.pallas import tpu as pltpu

F32 = jnp.float32
BF16 = jnp.bfloat16

D_MODEL = 2048
SEQ = 8192
DEPTH = 4
N_A_LAYERS = DEPTH // 2
EPS = 1e-6
D_FF = 5632
HEAD_DIM = 128
N_HEADS = 16
KV_LORA = 512
Q_LORA = 512
ROPE_DIM = 64
ROPE_THETA = 10000.0
QK_PAD = 256
WINDOW_KEYS = 128
RESIDUES = 16
NEG = -1e30

VMEM_LIMIT_BYTES = 56 * 1024 * 1024


def _params(*semantics):
    return pltpu.CompilerParams(dimension_semantics=semantics, vmem_limit_bytes=VMEM_LIMIT_BYTES)


def _rms(x, g):
    return x * lax.rsqrt(jnp.mean(x * x, axis=-1, keepdims=True) + EPS) * g


def _ffn_kernel(x_ref, g_ref, wg_ref, wu_ref, wd_ref, *rest, final):
    if final:
        fn_ref, o_ref, h_ref = rest
    else:
        o_ref, h_ref = rest
    f = pl.program_id(1)

    @pl.when(f == 0)
    def _():
        x = x_ref[...]
        h_ref[...] = _rms(x, g_ref[...]).astype(BF16)
        o_ref[...] = x

    h = h_ref[...]
    gate = jnp.dot(h, wg_ref[...].astype(BF16), preferred_element_type=F32)
    up = jnp.dot(h, wu_ref[...].astype(BF16), preferred_element_type=F32)
    act = (0.5 * (gate * jax.nn.sigmoid(gate)) * up).astype(BF16)
    o_ref[...] += jnp.dot(act, wd_ref[...].astype(BF16), preferred_element_type=F32)

    if final:
        @pl.when(f == pl.num_programs(1) - 1)
        def _():
            o_ref[...] = _rms(o_ref[...], fn_ref[...])


def _ffn(x, gain, wg, wu, wd, layer, final_gain=None, tm=1024, tf=256):
    S, D = x.shape
    F = wg.shape[-1]
    final = final_gain is not None
    in_specs = [
        pl.BlockSpec((tm, D), lambda i, f: (i, 0)),
        pl.BlockSpec((None, 1, D), lambda i, f: (layer, 0, 0)),
        pl.BlockSpec((None, D, tf), lambda i, f: (layer, 0, f)),
        pl.BlockSpec((None, D, tf), lambda i, f: (layer, 0, f)),
        pl.BlockSpec((None, tf, D), lambda i, f: (layer, f, 0)),
    ]
    args = [x, gain, wg, wu, wd]
    if final:
        in_specs.append(pl.BlockSpec((1, D), lambda i, f: (0, 0)))
        args.append(final_gain)
    return pl.pallas_call(
        functools.partial(_ffn_kernel, final=final),
        out_shape=jax.ShapeDtypeStruct((S, D), F32),
        grid=(S // tm, F // tf),
        in_specs=in_specs,
        out_specs=pl.BlockSpec((tm, D), lambda i, f: (i, 0)),
        scratch_shapes=[pltpu.VMEM((tm, D), BF16)],
        compiler_params=_params("parallel", "arbitrary"),
        name="ffn_final" if final else "ffn",
    )(*args)


def _norm_proj_kernel(x_ref, g_ref, w_ref, o_ref, h_ref, *, slabs):
    @pl.when(pl.program_id(1) == 0)
    def _():
        h_ref[...] = _rms(x_ref[...], g_ref[...]).astype(BF16)

    y = jnp.dot(h_ref[...], w_ref[...].astype(BF16), preferred_element_type=F32)
    if slabs:
        for s in range(slabs):
            o_ref[s] = y[:, s * HEAD_DIM:(s + 1) * HEAD_DIM]
    else:
        o_ref[...] = y


def _norm_proj(x, gain, gain_idx, w, w_idx, tn, slab_out, tm=1024):
    S, D = x.shape
    N = w.shape[-1]
    if slab_out:
        slabs = tn // HEAD_DIM
        out_shape = jax.ShapeDtypeStruct((N // HEAD_DIM, S, HEAD_DIM), F32)
        out_spec = pl.BlockSpec((slabs, tm, HEAD_DIM), lambda i, j: (j, i, 0))
    else:
        slabs = 0
        out_shape = jax.ShapeDtypeStruct((S, N), F32)
        out_spec = pl.BlockSpec((tm, tn), lambda i, j: (i, j))
    return pl.pallas_call(
        functools.partial(_norm_proj_kernel, slabs=slabs),
        out_shape=out_shape,
        grid=(S // tm, N // tn),
        in_specs=[
            pl.BlockSpec((tm, D), lambda i, j: (i, 0)),
            pl.BlockSpec((None, 1, D), lambda i, j: (gain_idx, 0, 0)),
            pl.BlockSpec((None, D, tn), lambda i, j: (w_idx, 0, j)),
        ],
        out_specs=out_spec,
        scratch_shapes=[pltpu.VMEM((tm, D), BF16)],
        compiler_params=_params("parallel", "arbitrary"),
        name="norm_proj_slab" if slab_out else "norm_proj",
    )(x, gain, w)


def _proj_res_kernel(a_ref, w_ref, x_ref, o_ref):
    o_ref[...] = x_ref[...] + jnp.dot(a_ref[...], w_ref[...].astype(BF16), preferred_element_type=F32)


def _proj_res(a, w, w_idx, x, tm=512, tn=2048):
    S, D = x.shape
    K = a.shape[-1]
    return pl.pallas_call(
        _proj_res_kernel,
        out_shape=jax.ShapeDtypeStruct((S, D), F32),
        grid=(S // tm, D // tn),
        in_specs=[
            pl.BlockSpec((tm, K), lambda i, j: (i, 0)),
            pl.BlockSpec((None, K, tn), lambda i, j: (w_idx, 0, j)),
            pl.BlockSpec((tm, tn), lambda i, j: (i, j)),
        ],
        out_specs=pl.BlockSpec((tm, tn), lambda i, j: (i, j)),
        compiler_params=_params("parallel", "arbitrary"),
        name="proj_res",
    )(a, w, x)


LOG2E = float(np.log2(np.e))


def _dil_bias(slope, dilation, steps):
    valid = (steps >= 0) & (steps <= WINDOW_KEYS)
    return jnp.where(valid, (-slope * dilation * LOG2E) * steps.astype(F32), NEG)


def _gather_rows(ref, chunks):
    parts = [ref[pl.ds(start, size), :] for start, size in chunks]
    return parts[0] if len(parts) == 1 else jnp.concatenate(parts, axis=0)


def _dil_tiles(q_ref, k_ref, v_ref, acc_ref, m_ref, tiles, first):
    old = []
    for q_chunks, _, _ in tiles:
        old.append(None if first else (_gather_rows(m_ref, q_chunks), _gather_rows(acc_ref, q_chunks)))
    new = []
    for (q_chunks, k_chunks, bias), state in zip(tiles, old):
        q = (_gather_rows(q_ref, q_chunks) * (HEAD_DIM ** -0.5 * LOG2E)).astype(BF16)
        k = _gather_rows(k_ref, k_chunks).astype(BF16)
        v = _gather_rows(v_ref, k_chunks).astype(BF16)
        v1 = jnp.concatenate([v, jnp.ones_like(v)], axis=1)
        s = lax.dot_general(q, k, (((1,), (1,)), ((), ())), preferred_element_type=F32) + bias
        m_tile = jnp.max(s, axis=-1, keepdims=True)
        if first:
            m_new = jnp.broadcast_to(m_tile, (WINDOW_KEYS, HEAD_DIM))
            p = jnp.exp2(s - m_tile)
            acc_new = jnp.dot(p.astype(BF16), v1, preferred_element_type=F32)
        else:
            m_old, acc_old = state
            m_new = jnp.maximum(m_old, m_tile)
            alpha = jnp.exp2(m_old - m_new)
            reps = s.shape[1] // HEAD_DIM
            m_wide = m_new if reps == 1 else jnp.concatenate([m_new] * reps, axis=1)
            p = jnp.exp2(s - m_wide)
            acc_new = (jnp.concatenate([alpha, alpha], axis=1) * acc_old
                       + jnp.dot(p.astype(BF16), v1, preferred_element_type=F32))
        new.append((m_new, acc_new))
    for (q_chunks, _, _), (m_new, acc_new) in zip(tiles, new):
        off = 0
        for start, size in q_chunks:
            m_ref[pl.ds(start, size), :] = m_new[off:off + size]
            acc_ref[pl.ds(start, size), :] = acc_new[off:off + size]
            off += size


def _dilated_kernel(slopes_ref, q_ref, k_ref, v_ref, o_ref, acc_ref, m_ref, *, batch1, batch2, batch3):
    S = q_ref.shape[0]
    A = S // RESIDUES
    n = WINDOW_KEYS
    slope = slopes_ref[pl.program_id(0)]
    run = functools.partial(_dil_tiles, q_ref, k_ref, v_ref, acc_ref, m_ref)

    def iota(shape, dim):
        return lax.broadcasted_iota(jnp.int32, shape, dim)

    def al(x):
        return x if isinstance(x, int) else pl.multiple_of(x, 8)

    qi, kj = iota((n, 2 * n), 0), iota((n, 2 * n), 1)
    steps = n + 16 * ((qi & 7) - (kj & 15)) + ((qi >> 3) - (kj >> 4))
    bias1 = _dil_bias(slope, 1.0, steps)
    qi, kj = iota((n, n), 0), iota((n, n), 1)
    steps = 16 * ((qi & 7) - (kj & 7)) + ((qi >> 3) - (kj >> 3))
    bias1_first = _dil_bias(slope, 1.0, steps)

    def b1(c, has_prev):
        qc = [(al(r * A + c * 8), 8) for r in range(RESIDUES)]
        if has_prev:
            kc = [(al(r * A + (c - 1) * 8), 16) for r in range(RESIDUES)]
        else:
            kc = qc
        return qc, kc, bias1 if has_prev else bias1_first

    run([b1(0, False)], True)

    def b1_body(g, carry):
        run([b1(1 + g * batch1 + t, True) for t in range(batch1)], True)
        return carry

    lax.fori_loop(0, (S // n - 1) // batch1, b1_body, 0)

    qi, kj = iota((n, 2 * n), 0), iota((n, 2 * n), 1)
    steps = n + 4 * ((qi & 31) - (kj & 63)) + ((qi >> 5) - (kj >> 6))
    bias2 = _dil_bias(slope, 4.0, steps)
    qi, kj = iota((n, n), 0), iota((n, n), 1)
    steps = 4 * ((qi & 31) - (kj & 31)) + ((qi >> 5) - (kj >> 5))
    bias2_first = _dil_bias(slope, 4.0, steps)

    def b2(r4, c, has_prev):
        qc = [(al((r4 + 4 * s) * A + c * 32), 32) for s in range(4)]
        if has_prev:
            kc = [(al((r4 + 4 * s) * A + (c - 1) * 32), 64) for s in range(4)]
        else:
            kc = qc
        return qc, kc, bias2 if has_prev else bias2_first

    def b2_outer(r4, carry):
        def b2_body(g, carry2):
            run([b2(r4, 1 + g * batch2 + t, True) for t in range(batch2)], False)
            return carry2

        run([b2(r4, 0, False)], False)
        lax.fori_loop(0, (S // (4 * n) - 1) // batch2, b2_body, 0)
        return carry

    lax.fori_loop(0, 4, b2_outer, 0)

    qi, kj = iota((n, 2 * n), 0), iota((n, 2 * n), 1)
    bias3 = _dil_bias(slope, 16.0, n + qi - kj)
    qi, kj = iota((n, n), 0), iota((n, n), 1)
    bias3_first = _dil_bias(slope, 16.0, qi - kj)

    def b3_body(g, carry):
        tiles = []
        for t in range(batch3):
            base = (batch3 * g + t) * A
            tiles.append(([(al(base), n)], [(al(base), n)], bias3_first))
            for c in range(1, A // n):
                tiles.append(([(al(base + c * n), n)], [(al(base + (c - 1) * n), 2 * n)], bias3))
        run(tiles, False)
        return carry

    lax.fori_loop(0, RESIDUES // batch3, b3_body, 0)

    o_ref[...] = (acc_ref[:, :HEAD_DIM] / acc_ref[:, HEAD_DIM:]).astype(o_ref.dtype)


def _dilated_attention(qkv, slopes):
    S = qkv.shape[1]
    H = N_HEADS

    def slab(offset):
        return pl.BlockSpec((None, S, HEAD_DIM), lambda h: (offset + h, 0, 0))

    return pl.pallas_call(
        functools.partial(_dilated_kernel, batch1=9, batch2=15, batch3=4),
        out_shape=jax.ShapeDtypeStruct((S, H * HEAD_DIM), BF16),
        grid=(H,),
        in_specs=[pl.BlockSpec(memory_space=pltpu.SMEM), slab(0), slab(H), slab(2 * H)],
        out_specs=pl.BlockSpec((S, HEAD_DIM), lambda h: (0, h)),
        scratch_shapes=[pltpu.VMEM((S, 2 * HEAD_DIM), F32), pltpu.VMEM((S, HEAD_DIM), F32)],
        compiler_params=_params("parallel"),
        name="dilated_attn",
    )(slopes, qkv, qkv, qkv)


def _rope_pad(z, cos_ref, sin_ref):
    return z * cos_ref[...] + pltpu.roll(z, ROPE_DIM, axis=1) * sin_ref[...]


def _mla_q_kernel(t_ref, g_ref, w_ref, cos_ref, sin_ref, o_ref, cq_ref, *, heads, scale):
    @pl.when(pl.program_id(1) == 0)
    def _():
        cq_ref[...] = _rms(t_ref[...], g_ref[...]).astype(BF16)

    y = jnp.dot(cq_ref[...], w_ref[...], preferred_element_type=F32)
    for hh in range(heads):
        base = hh * QK_PAD
        o_ref[hh, :, :HEAD_DIM] = (y[:, base:base + HEAD_DIM] * scale).astype(BF16)
        z = y[:, base + HEAD_DIM:base + QK_PAD]
        o_ref[hh, :, HEAD_DIM:] = (_rope_pad(z, cos_ref, sin_ref) * scale).astype(BF16)


def _mla_q(t, gain, gain_idx, w, w_idx, cos_pad, sin_pad, scale, tm=1024, heads=8):
    S, C = t.shape
    return pl.pallas_call(
        functools.partial(_mla_q_kernel, heads=heads, scale=scale),
        out_shape=jax.ShapeDtypeStruct((N_HEADS, S, QK_PAD), BF16),
        grid=(S // tm, N_HEADS // heads),
        in_specs=[
            pl.BlockSpec((tm, C), lambda i, j: (i, 0)),
            pl.BlockSpec((None, 1, C), lambda i, j: (gain_idx, 0, 0)),
            pl.BlockSpec((None, C, heads * QK_PAD), lambda i, j: (w_idx, 0, j)),
            pl.BlockSpec((tm, HEAD_DIM), lambda i, j: (i, 0)),
            pl.BlockSpec((tm, HEAD_DIM), lambda i, j: (i, 0)),
        ],
        out_specs=pl.BlockSpec((heads, tm, QK_PAD), lambda i, j: (j, i, 0)),
        scratch_shapes=[pltpu.VMEM((tm, C), BF16)],
        compiler_params=_params("parallel", "arbitrary"),
        name="mla_q",
    )(t, gain, w, cos_pad, sin_pad)


def _mla_kv_kernel(t_ref, g_ref, wuk_ref, wuv_ref, cos_ref, sin_ref, k_ref, v_ref):
    ckv = _rms(t_ref[:, :KV_LORA], g_ref[...]).astype(BF16)
    k_nope = jnp.dot(ckv, wuk_ref[...], preferred_element_type=F32)
    v = jnp.dot(ckv, wuv_ref[...], preferred_element_type=F32)
    k_rope = _rope_pad(t_ref[:, KV_LORA:], cos_ref, sin_ref).astype(BF16)
    ones = jnp.ones((t_ref.shape[0], HEAD_DIM), BF16)
    for h in range(N_HEADS):
        cols = slice(h * HEAD_DIM, (h + 1) * HEAD_DIM)
        k_ref[h, :, :HEAD_DIM] = k_nope[:, cols].astype(BF16)
        k_ref[h, :, HEAD_DIM:] = k_rope
        v_ref[h, :, :HEAD_DIM] = v[:, cols].astype(BF16)
        v_ref[h, :, HEAD_DIM:] = ones


def _mla_kv(t, gain, wuk, wuv, cos_pad, sin_pad, tm=512):
    S, W = t.shape
    HD = N_HEADS * HEAD_DIM
    return pl.pallas_call(
        _mla_kv_kernel,
        out_shape=(jax.ShapeDtypeStruct((N_HEADS, S, QK_PAD), BF16),
                   jax.ShapeDtypeStruct((N_HEADS, S, 2 * HEAD_DIM), BF16)),
        grid=(S // tm,),
        in_specs=[
            pl.BlockSpec((tm, W), lambda i: (i, 0)),
            pl.BlockSpec((1, KV_LORA), lambda i: (0, 0)),
            pl.BlockSpec((KV_LORA, HD), lambda i: (0, 0)),
            pl.BlockSpec((KV_LORA, HD), lambda i: (0, 0)),
            pl.BlockSpec((tm, HEAD_DIM), lambda i: (i, 0)),
            pl.BlockSpec((tm, HEAD_DIM), lambda i: (i, 0)),
        ],
        out_specs=(pl.BlockSpec((N_HEADS, tm, QK_PAD), lambda i: (0, i, 0)),
                   pl.BlockSpec((N_HEADS, tm, 2 * HEAD_DIM), lambda i: (0, i, 0))),
        compiler_params=_params("parallel"),
        name="mla_kv",
    )(t, gain, wuk, wuv, cos_pad, sin_pad)


def _mla_attn_kernel(q_ref, k_ref, v_ref, o_ref, s_ref, bmax_ref, m_ref, acc_ref, *, blk):
    i = pl.program_id(1)
    half = blk // 2
    nt = (((1,), (1,)), ((), ()))

    def bands(diag):
        return [(0, half, half), (half, half, blk)] if diag else [(0, blk, blk)]

    def scores(j, slot, diag):
        start = pl.multiple_of(j * blk, blk)
        for r0, nr, nc in bands(diag):
            s = lax.dot_general(q_ref[r0:r0 + nr, :], k_ref[pl.ds(start, nc), :], nt, preferred_element_type=F32)
            if diag:
                row = lax.broadcasted_iota(jnp.int32, s.shape, 0) + r0
                col = lax.broadcasted_iota(jnp.int32, s.shape, 1)
                s = jnp.where(col <= row, s, NEG)
            s_ref[slot, r0:r0 + nr, :nc] = s
            bmax_ref[slot, r0:r0 + nr, :] = jnp.broadcast_to(jnp.max(s, axis=-1, keepdims=True), (nr, HEAD_DIM))

    def update(j, slot, diag=False):
        start = pl.multiple_of(j * blk, blk)
        for r0, nr, nc in bands(diag):
            r = slice(r0, r0 + nr)
            m_old = m_ref[r, :]
            m_new = jnp.maximum(m_old, bmax_ref[slot, r, :])
            alpha = jnp.exp2(m_old - m_new)
            p = jnp.exp2(s_ref[slot, r, :nc] - jnp.concatenate([m_new] * (nc // HEAD_DIM), axis=1))
            acc_ref[r, :] = (jnp.concatenate([alpha, alpha], axis=1) * acc_ref[r, :]
                             + jnp.dot(p.astype(BF16), v_ref[pl.ds(start, nc), :], preferred_element_type=F32))
            m_ref[r, :] = m_new

    m_ref[...] = jnp.full(m_ref.shape, NEG, F32)
    acc_ref[...] = jnp.zeros(acc_ref.shape, F32)

    @pl.when(i == 0)
    def _():
        scores(0, 0, True)

    @pl.when(i > 0)
    def _():
        scores(0, 0, False)

    def pair(jj, carry):
        j = 2 * jj
        scores(j + 1, 1, False)
        update(j, 0)
        scores(j + 2, 0, False)
        update(j + 1, 1)
        return carry

    lax.fori_loop(0, (i - 1) // 2, pair, 0)

    @pl.when((i > 0) & (i % 2 == 0))
    def _():
        scores(i - 1, 1, False)
        update(i - 2, 0)
        scores(i, 0, True)
        update(i - 1, 1)
        update(i, 0, True)

    @pl.when(i % 2 == 1)
    def _():
        scores(i, 1, True)
        update(i - 1, 0)
        update(i, 1, True)

    @pl.when(i == 0)
    def _():
        update(0, 0, True)

    o_ref[...] = (acc_ref[:, :HEAD_DIM] / acc_ref[:, HEAD_DIM:]).astype(o_ref.dtype)


def _mla_attention(q, k, v, blk=1024):
    H, S, _ = q.shape
    return pl.pallas_call(
        functools.partial(_mla_attn_kernel, blk=blk),
        out_shape=jax.ShapeDtypeStruct((S, H * HEAD_DIM), BF16),
        grid=(H, S // blk),
        in_specs=[
            pl.BlockSpec((None, blk, QK_PAD), lambda h, i: (h, i, 0)),
            pl.BlockSpec((None, S, QK_PAD), lambda h, i: (h, 0, 0)),
            pl.BlockSpec((None, S, 2 * HEAD_DIM), lambda h, i: (h, 0, 0)),
        ],
        out_specs=pl.BlockSpec((blk, HEAD_DIM), lambda h, i: (i, h)),
        scratch_shapes=[
            pltpu.VMEM((2, blk, blk), F32),
            pltpu.VMEM((2, blk, HEAD_DIM), F32),
            pltpu.VMEM((blk, HEAD_DIM), F32),
            pltpu.VMEM((blk, 2 * HEAD_DIM), F32),
        ],
        compiler_params=_params("parallel", "arbitrary"),
        name="mla_attn",
    )(q, k, v)


def _rope_tables(seq):
    inv = 1.0 / (ROPE_THETA ** (jnp.arange(0, ROPE_DIM, 2, dtype=F32) / ROPE_DIM))
    ang = jnp.arange(seq, dtype=F32)[:, None] * inv[None, :]
    cos, sin = jnp.cos(ang), jnp.sin(ang)
    zeros = jnp.zeros((seq, HEAD_DIM - ROPE_DIM), F32)
    cos_pad = jnp.concatenate([cos, cos, zeros], axis=1)
    sin_pad = jnp.concatenate([-sin, sin, zeros], axis=1)
    return cos_pad, sin_pad


def kernel(x, ffn_norm1, ffn1_wg, ffn1_wu, ffn1_wd, mix_norm, ffn_norm2, ffn2_wg, ffn2_wu, ffn2_wd,
           a_wqkv, a_wo, kv_norm, b_wdkv, b_ckv_norm, b_wkr, b_wuk, b_wuv,
           b_wdq, b_cq_norm, b_wuq, b_wo, final_norm):
    B, S, D = x.shape
    assert (B, S, D) == (1, SEQ, D_MODEL)
    bf = lambda w: w.astype(BF16)
    gains = lambda g: g.reshape(g.shape[0], 1, g.shape[-1])

    w1g, w1u, w1d = ffn1_wg, ffn1_wu, ffn1_wd
    w2g, w2u, w2d = ffn2_wg, ffn2_wu, ffn2_wd
    wqkv, wo_a, wo_b = a_wqkv, a_wo, b_wo
    g1, g2, gm = gains(ffn_norm1), gains(ffn_norm2), gains(mix_norm)

    rot = (np.arange(ROPE_DIM) + ROPE_DIM // 2) % ROPE_DIM
    w_kv_down = bf(jnp.concatenate([b_wdkv, b_wkr, b_wkr[:, rot]], axis=1))[None]
    wuk = bf(b_wuk.reshape(KV_LORA, N_HEADS * HEAD_DIM))
    wuv = bf(b_wuv.reshape(KV_LORA, N_HEADS * HEAD_DIM))
    wdq = bf(b_wdq)
    q_rope_w = b_wuq[..., HEAD_DIM:]
    wuq = bf(jnp.concatenate([b_wuq, q_rope_w[..., rot]], axis=-1).reshape(
        b_wuq.shape[0], Q_LORA, N_HEADS * QK_PAD))

    slopes = jnp.asarray(2.0 ** (-8.0 * (np.arange(N_HEADS) + 1) / N_HEADS), dtype=F32)
    cos_pad, sin_pad = _rope_tables(S)
    scale = float((HEAD_DIM + ROPE_DIM) ** -0.5 * np.log2(np.e))

    xs = x[0]
    xs = xs.reshape(S // RESIDUES, RESIDUES, D).transpose(1, 0, 2).reshape(S, D)
    for layer in range(N_A_LAYERS):
        xs = _ffn(xs, g1, w1g, w1u, w1d, layer)
        qkv = _norm_proj(xs, gm, layer, wqkv, layer, tn=256, slab_out=True, tm=2048)
        o = _dilated_attention(qkv, slopes)
        xs = _proj_res(o, wo_a, layer, xs)
        xs = _ffn(xs, g2, w2g, w2u, w2d, layer)
    xs = xs.reshape(RESIDUES, S // RESIDUES, D).transpose(1, 0, 2).reshape(S, D)

    t_kv = _norm_proj(xs, kv_norm.reshape(1, 1, D), 0, w_kv_down, 0, tn=w_kv_down.shape[-1], slab_out=False)
    k_all, v_all = _mla_kv(t_kv, b_ckv_norm.reshape(1, KV_LORA), wuk, wuv, cos_pad, sin_pad)
    gq = gains(b_cq_norm)
    for layer in range(N_A_LAYERS, DEPTH):
        jb = layer - N_A_LAYERS
        xs = _ffn(xs, g1, w1g, w1u, w1d, layer)
        t_q = _norm_proj(xs, gm, layer, wdq, jb, tn=Q_LORA, slab_out=False)
        q_all = _mla_q(t_q, gq, jb, wuq, jb, cos_pad, sin_pad, scale)
        o = _mla_attention(q_all, k_all, v_all)
        xs = _proj_res(o, wo_b, jb, xs)
        last = layer == DEPTH - 1
        xs = _ffn(xs, g2, w2g, w2u, w2d, layer, final_gain=final_norm.reshape(1, D) if last else None)
    return xs[None]
```

```python
import functools

import numpy as np
import jax
import jax.numpy as jnp
from jax import lax
from jax.experimental import pallas as pl
from jax.experimental.pallas import tpu as pltpu

F32 = jnp.float32
BF16 = jnp.bfloat16

D_MODEL = 2048
SEQ = 8192
DEPTH = 4
N_A_LAYERS = DEPTH // 2
EPS = 1e-6
D_FF = 5632
HEAD_DIM = 128
N_HEADS = 16
KV_LORA = 512
Q_LORA = 512
ROPE_DIM = 64
ROPE_THETA = 10000.0
QK_PAD = 256
WINDOW_KEYS = 128
RESIDUES = 16
NEG = -1e30

VMEM_LIMIT_BYTES = 56 * 1024 * 1024


def _params(*semantics):
    return pltpu.CompilerParams(dimension_semantics=semantics, vmem_limit_bytes=VMEM_LIMIT_BYTES)


def _rms(x, g):
    return x * lax.rsqrt(jnp.mean(x * x, axis=-1, keepdims=True) + EPS) * g


def _ffn_kernel(x_ref, g_ref, wg_ref, wu_ref, wd_ref, *rest, final):
    if final:
        fn_ref, o_ref, h_ref = rest
    else:
        o_ref, h_ref = rest
    f = pl.program_id(1)

    @pl.when(f == 0)
    def _():
        x = x_ref[...]
        h_ref[...] = _rms(x, g_ref[...]).astype(BF16)
        o_ref[...] = x

    h = h_ref[...]
    gate = jnp.dot(h, wg_ref[...].astype(BF16), preferred_element_type=F32)
    up = jnp.dot(h, wu_ref[...].astype(BF16), preferred_element_type=F32)
    act = (0.5 * (gate * jax.nn.sigmoid(gate)) * up).astype(BF16)
    o_ref[...] += jnp.dot(act, wd_ref[...].astype(BF16), preferred_element_type=F32)

    if final:
        @pl.when(f == pl.num_programs(1) - 1)
        def _():
            o_ref[...] = _rms(o_ref[...], fn_ref[...])


def _ffn(x, gain, wg, wu, wd, layer, final_gain=None, tm=1024, tf=256):
    S, D = x.shape
    F = wg.shape[-1]
    final = final_gain is not None
    in_specs = [
        pl.BlockSpec((tm, D), lambda i, f: (i, 0)),
        pl.BlockSpec((None, 1, D), lambda i, f: (layer, 0, 0)),
        pl.BlockSpec((None, D, tf), lambda i, f: (layer, 0, f)),
        pl.BlockSpec((None, D, tf), lambda i, f: (layer, 0, f)),
        pl.BlockSpec((None, tf, D), lambda i, f: (layer, f, 0)),
    ]
    args = [x, gain, wg, wu, wd]
    if final:
        in_specs.append(pl.BlockSpec((1, D), lambda i, f: (0, 0)))
        args.append(final_gain)
    return pl.pallas_call(
        functools.partial(_ffn_kernel, final=final),
        out_shape=jax.ShapeDtypeStruct((S, D), F32),
        grid=(S // tm, F // tf),
        in_specs=in_specs,
        out_specs=pl.BlockSpec((tm, D), lambda i, f: (i, 0)),
        scratch_shapes=[pltpu.VMEM((tm, D), BF16)],
        compiler_params=_params("parallel", "arbitrary"),
        name="ffn_final" if final else "ffn",
    )(*args)


def _norm_proj_kernel(x_ref, g_ref, w_ref, o_ref, h_ref, *, slabs):
    @pl.when(pl.program_id(1) == 0)
    def _():
        h_ref[...] = _rms(x_ref[...], g_ref[...]).astype(BF16)

    y = jnp.dot(h_ref[...], w_ref[...].astype(BF16), preferred_element_type=F32)
    if slabs:
        for s in range(slabs):
            o_ref[s] = y[:, s * HEAD_DIM:(s + 1) * HEAD_DIM]
    else:
        o_ref[...] = y


def _norm_proj(x, gain, gain_idx, w, w_idx, tn, slab_out, tm=1024):
    S, D = x.shape
    N = w.shape[-1]
    if slab_out:
        slabs = tn // HEAD_DIM
        out_shape = jax.ShapeDtypeStruct((N // HEAD_DIM, S, HEAD_DIM), F32)
        out_spec = pl.BlockSpec((slabs, tm, HEAD_DIM), lambda i, j: (j, i, 0))
    else:
        slabs = 0
        out_shape = jax.ShapeDtypeStruct((S, N), F32)
        out_spec = pl.BlockSpec((tm, tn), lambda i, j: (i, j))
    return pl.pallas_call(
        functools.partial(_norm_proj_kernel, slabs=slabs),
        out_shape=out_shape,
        grid=(S // tm, N // tn),
        in_specs=[
            pl.BlockSpec((tm, D), lambda i, j: (i, 0)),
            pl.BlockSpec((None, 1, D), lambda i, j: (gain_idx, 0, 0)),
            pl.BlockSpec((None, D, tn), lambda i, j: (w_idx, 0, j)),
        ],
        out_specs=out_spec,
        scratch_shapes=[pltpu.VMEM((tm, D), BF16)],
        compiler_params=_params("parallel", "arbitrary"),
        name="norm_proj_slab" if slab_out else "norm_proj",
    )(x, gain, w)


def _proj_res_kernel(a_ref, w_ref, x_ref, o_ref):
    o_ref[...] = x_ref[...] + jnp.dot(a_ref[...], w_ref[...].astype(BF16), preferred_element_type=F32)


def _proj_res(a, w, w_idx, x, tm=512, tn=2048):
    S, D = x.shape
    K = a.shape[-1]
    return pl.pallas_call(
        _proj_res_kernel,
        out_shape=jax.ShapeDtypeStruct((S, D), F32),
        grid=(S // tm, D // tn),
        in_specs=[
            pl.BlockSpec((tm, K), lambda i, j: (i, 0)),
            pl.BlockSpec((None, K, tn), lambda i, j: (w_idx, 0, j)),
            pl.BlockSpec((tm, tn), lambda i, j: (i, j)),
        ],
        out_specs=pl.BlockSpec((tm, tn), lambda i, j: (i, j)),
        compiler_params=_params("parallel", "arbitrary"),
        name="proj_res",
    )(a, w, x)


LOG2E = float(np.log2(np.e))


def _dil_bias(slope, dilation, steps):
    valid = (steps >= 0) & (steps <= WINDOW_KEYS)
    return jnp.where(valid, (-slope * dilation * LOG2E) * steps.astype(F32), NEG)


def _gather_rows(ref, chunks):
    parts = [ref[pl.ds(start, size), :] for start, size in chunks]
    return parts[0] if len(parts) == 1 else jnp.concatenate(parts, axis=0)


def _dil_tiles(q_ref, k_ref, v_ref, acc_ref, m_ref, tiles, first):
    old = []
    for q_chunks, _, _ in tiles:
        old.append(None if first else (_gather_rows(m_ref, q_chunks), _gather_rows(acc_ref, q_chunks)))
    new = []
    for (q_chunks, k_chunks, bias), state in zip(tiles, old):
        q = (_gather_rows(q_ref, q_chunks) * (HEAD_DIM ** -0.5 * LOG2E)).astype(BF16)
        k = _gather_rows(k_ref, k_chunks).astype(BF16)
        v = _gather_rows(v_ref, k_chunks).astype(BF16)
        v1 = jnp.concatenate([v, jnp.ones_like(v)], axis=1)
        s = lax.dot_general(q, k, (((1,), (1,)), ((), ())), preferred_element_type=F32) + bias
        m_tile = jnp.max(s, axis=-1, keepdims=True)
        if first:
            m_new = jnp.broadcast_to(m_tile, (WINDOW_KEYS, HEAD_DIM))
            p = jnp.exp2(s - m_tile)
            acc_new = jnp.dot(p.astype(BF16), v1, preferred_element_type=F32)
        else:
            m_old, acc_old = state
            m_new = jnp.maximum(m_old, m_tile)
            alpha = jnp.exp2(m_old - m_new)
            reps = s.shape[1] // HEAD_DIM
            m_wide = m_new if reps == 1 else jnp.concatenate([m_new] * reps, axis=1)
            p = jnp.exp2(s - m_wide)
            acc_new = (jnp.concatenate([alpha, alpha], axis=1) * acc_old
                       + jnp.dot(p.astype(BF16), v1, preferred_element_type=F32))
        new.append((m_new, acc_new))
    for (q_chunks, _, _), (m_new, acc_new) in zip(tiles, new):
        off = 0
        for start, size in q_chunks:
            m_ref[pl.ds(start, size), :] = m_new[off:off + size]
            acc_ref[pl.ds(start, size), :] = acc_new[off:off + size]
            off += size


def _dilated_kernel(slopes_ref, q_ref, k_ref, v_ref, o_ref, acc_ref, m_ref, *, batch1, batch2, batch3):
    S = q_ref.shape[0]
    A = S // RESIDUES
    n = WINDOW_KEYS
    slope = slopes_ref[pl.program_id(0)]
    run = functools.partial(_dil_tiles, q_ref, k_ref, v_ref, acc_ref, m_ref)

    def iota(shape, dim):
        return lax.broadcasted_iota(jnp.int32, shape, dim)

    def al(x):
        return x if isinstance(x, int) else pl.multiple_of(x, 8)

    qi, kj = iota((n, 2 * n), 0), iota((n, 2 * n), 1)
    steps = n + 16 * ((qi & 7) - (kj & 15)) + ((qi >> 3) - (kj >> 4))
    bias1 = _dil_bias(slope, 1.0, steps)
    qi, kj = iota((n, n), 0), iota((n, n), 1)
    steps = 16 * ((qi & 7) - (kj & 7)) + ((qi >> 3) - (kj >> 3))
    bias1_first = _dil_bias(slope, 1.0, steps)

    def b1(c, has_prev):
        qc = [(al(r * A + c * 8), 8) for r in range(RESIDUES)]
        if has_prev:
            kc = [(al(r * A + (c - 1) * 8), 16) for r in range(RESIDUES)]
        else:
            kc = qc
        return qc, kc, bias1 if has_prev else bias1_first

    run([b1(0, False)], True)

    def b1_body(g, carry):
        run([b1(1 + g * batch1 + t, True) for t in range(batch1)], True)
        return carry

    lax.fori_loop(0, (S // n - 1) // batch1, b1_body, 0)

    qi, kj = iota((n, 2 * n), 0), iota((n, 2 * n), 1)
    steps = n + 4 * ((qi & 31) - (kj & 63)) + ((qi >> 5) - (kj >> 6))
    bias2 = _dil_bias(slope, 4.0, steps)
    qi, kj = iota((n, n), 0), iota((n, n), 1)
    steps = 4 * ((qi & 31) - (kj & 31)) + ((qi >> 5) - (kj >> 5))
    bias2_first = _dil_bias(slope, 4.0, steps)

    def b2(r4, c, has_prev):
        qc = [(al((r4 + 4 * s) * A + c * 32), 32) for s in range(4)]
        if has_prev:
            kc = [(al((r4 + 4 * s) * A + (c - 1) * 32), 64) for s in range(4)]
        else:
            kc = qc
        return qc, kc, bias2 if has_prev else bias2_first

    def b2_outer(r4, carry):
        def b2_body(g, carry2):
            run([b2(r4, 1 + g * batch2 + t, True) for t in range(batch2)], False)
            return carry2

        run([b2(r4, 0, False)], False)
        lax.fori_loop(0, (S // (4 * n) - 1) // batch2, b2_body, 0)
        return carry

    lax.fori_loop(0, 4, b2_outer, 0)

    qi, kj = iota((n, 2 * n), 0), iota((n, 2 * n), 1)
    bias3 = _dil_bias(slope, 16.0, n + qi - kj)
    qi, kj = iota((n, n), 0), iota((n, n), 1)
    bias3_first = _dil_bias(slope, 16.0, qi - kj)

    def b3_body(g, carry):
        tiles = []
        for t in range(batch3):
            base = (batch3 * g + t) * A
            tiles.append(([(al(base), n)], [(al(base), n)], bias3_first))
            for c in range(1, A // n):
                tiles.append(([(al(base + c * n), n)], [(al(base + (c - 1) * n), 2 * n)], bias3))
        run(tiles, False)
        return carry

    lax.fori_loop(0, RESIDUES // batch3, b3_body, 0)

    o_ref[...] = (acc_ref[:, :HEAD_DIM] / acc_ref[:, HEAD_DIM:]).astype(o_ref.dtype)


def _dilated_attention(qkv, slopes):
    S = qkv.shape[1]
    H = N_HEADS

    def slab(offset):
        return pl.BlockSpec((None, S, HEAD_DIM), lambda h: (offset + h, 0, 0))

    return pl.pallas_call(
        functools.partial(_dilated_kernel, batch1=21, batch2=15, batch3=8),
        out_shape=jax.ShapeDtypeStruct((S, H * HEAD_DIM), BF16),
        grid=(H,),
        in_specs=[pl.BlockSpec(memory_space=pltpu.SMEM), slab(0), slab(H), slab(2 * H)],
        out_specs=pl.BlockSpec((S, HEAD_DIM), lambda h: (0, h)),
        scratch_shapes=[pltpu.VMEM((S, 2 * HEAD_DIM), F32), pltpu.VMEM((S, HEAD_DIM), F32)],
        compiler_params=_params("parallel"),
        name="dilated_attn",
    )(slopes, qkv, qkv, qkv)


def _rope_pad(z, cos_ref, sin_ref):
    return z * cos_ref[...] + pltpu.roll(z, ROPE_DIM, axis=1) * sin_ref[...]


def _mla_q_kernel(t_ref, g_ref, w_ref, cos_ref, sin_ref, o_ref, cq_ref, *, heads, scale):
    @pl.when(pl.program_id(1) == 0)
    def _():
        cq_ref[...] = _rms(t_ref[...], g_ref[...]).astype(BF16)

    y = jnp.dot(cq_ref[...], w_ref[...], preferred_element_type=F32)
    for hh in range(heads):
        base = hh * QK_PAD
        o_ref[hh, :, :HEAD_DIM] = (y[:, base:base + HEAD_DIM] * scale).astype(BF16)
        z = y[:, base + HEAD_DIM:base + QK_PAD]
        o_ref[hh, :, HEAD_DIM:] = (_rope_pad(z, cos_ref, sin_ref) * scale).astype(BF16)


def _mla_q(t, gain, gain_idx, w, w_idx, cos_pad, sin_pad, scale, tm=1024, heads=8):
    S, C = t.shape
    return pl.pallas_call(
        functools.partial(_mla_q_kernel, heads=heads, scale=scale),
        out_shape=jax.ShapeDtypeStruct((N_HEADS, S, QK_PAD), BF16),
        grid=(S // tm, N_HEADS // heads),
        in_specs=[
            pl.BlockSpec((tm, C), lambda i, j: (i, 0)),
            pl.BlockSpec((None, 1, C), lambda i, j: (gain_idx, 0, 0)),
            pl.BlockSpec((None, C, heads * QK_PAD), lambda i, j: (w_idx, 0, j)),
            pl.BlockSpec((tm, HEAD_DIM), lambda i, j: (i, 0)),
            pl.BlockSpec((tm, HEAD_DIM), lambda i, j: (i, 0)),
        ],
        out_specs=pl.BlockSpec((heads, tm, QK_PAD), lambda i, j: (j, i, 0)),
        scratch_shapes=[pltpu.VMEM((tm, C), BF16)],
        compiler_params=_params("parallel", "arbitrary"),
        name="mla_q",
    )(t, gain, w, cos_pad, sin_pad)


def _mla_kv_kernel(t_ref, g_ref, wuk_ref, wuv_ref, cos_ref, sin_ref, k_ref, v_ref):
    ckv = _rms(t_ref[:, :KV_LORA], g_ref[...]).astype(BF16)
    k_nope = jnp.dot(ckv, wuk_ref[...], preferred_element_type=F32)
    v = jnp.dot(ckv, wuv_ref[...], preferred_element_type=F32)
    k_rope = _rope_pad(t_ref[:, KV_LORA:], cos_ref, sin_ref).astype(BF16)
    ones = jnp.ones((t_ref.shape[0], HEAD_DIM), BF16)
    for h in range(N_HEADS):
        cols = slice(h * HEAD_DIM, (h + 1) * HEAD_DIM)
        k_ref[h, :, :HEAD_DIM] = k_nope[:, cols].astype(BF16)
        k_ref[h, :, HEAD_DIM:] = k_rope
        v_ref[h, :, :HEAD_DIM] = v[:, cols].astype(BF16)
        v_ref[h, :, HEAD_DIM:] = ones


def _mla_kv(t, gain, wuk, wuv, cos_pad, sin_pad, tm=512):
    S, W = t.shape
    HD = N_HEADS * HEAD_DIM
    return pl.pallas_call(
        _mla_kv_kernel,
        out_shape=(jax.ShapeDtypeStruct((N_HEADS, S, QK_PAD), BF16),
                   jax.ShapeDtypeStruct((N_HEADS, S, 2 * HEAD_DIM), BF16)),
        grid=(S // tm,),
        in_specs=[
            pl.BlockSpec((tm, W), lambda i: (i, 0)),
            pl.BlockSpec((1, KV_LORA), lambda i: (0, 0)),
            pl.BlockSpec((KV_LORA, HD), lambda i: (0, 0)),
            pl.BlockSpec((KV_LORA, HD), lambda i: (0, 0)),
            pl.BlockSpec((tm, HEAD_DIM), lambda i: (i, 0)),
            pl.BlockSpec((tm, HEAD_DIM), lambda i: (i, 0)),
        ],
        out_specs=(pl.BlockSpec((N_HEADS, tm, QK_PAD), lambda i: (0, i, 0)),
                   pl.BlockSpec((N_HEADS, tm, 2 * HEAD_DIM), lambda i: (0, i, 0))),
        compiler_params=_params("parallel"),
        name="mla_kv",
    )(t, gain, wuk, wuv, cos_pad, sin_pad)


def _mla_attn_kernel(q_ref, k_ref, v_ref, o_ref, s_ref, bmax_ref, m_ref, acc_ref, *, blk):
    i = pl.program_id(1)
    half = blk // 2
    nt = (((1,), (1,)), ((), ()))

    def bands(diag):
        return [(0, half, half), (half, half, blk)] if diag else [(0, blk, blk)]

    def scores(j, slot, diag):
        start = pl.multiple_of(j * blk, blk)
        for r0, nr, nc in bands(diag):
            s = lax.dot_general(q_ref[r0:r0 + nr, :], k_ref[pl.ds(start, nc), :], nt, preferred_element_type=F32)
            if diag:
                row = lax.broadcasted_iota(jnp.int32, s.shape, 0) + r0
                col = lax.broadcasted_iota(jnp.int32, s.shape, 1)
                s = jnp.where(col <= row, s, NEG)
            s_ref[slot, r0:r0 + nr, :nc] = s
            bmax_ref[slot, r0:r0 + nr, :] = jnp.broadcast_to(jnp.max(s, axis=-1, keepdims=True), (nr, HEAD_DIM))

    def update(j, slot, diag=False):
        start = pl.multiple_of(j * blk, blk)
        for r0, nr, nc in bands(diag):
            r = slice(r0, r0 + nr)
            m_old = m_ref[r, :]
            m_new = jnp.maximum(m_old, bmax_ref[slot, r, :])
            alpha = jnp.exp2(m_old - m_new)
            p = jnp.exp2(s_ref[slot, r, :nc] - jnp.concatenate([m_new] * (nc // HEAD_DIM), axis=1))
            acc_ref[r, :] = (jnp.concatenate([alpha, alpha], axis=1) * acc_ref[r, :]
                             + jnp.dot(p.astype(BF16), v_ref[pl.ds(start, nc), :], preferred_element_type=F32))
            m_ref[r, :] = m_new

    m_ref[...] = jnp.full(m_ref.shape, NEG, F32)
    acc_ref[...] = jnp.zeros(acc_ref.shape, F32)

    @pl.when(i == 0)
    def _():
        scores(0, 0, True)

    @pl.when(i > 0)
    def _():
        scores(0, 0, False)

    def pair(jj, carry):
        j = 2 * jj
        scores(j + 1, 1, False)
        update(j, 0)
        scores(j + 2, 0, False)
        update(j + 1, 1)
        return carry

    lax.fori_loop(0, (i - 1) // 2, pair, 0)

    @pl.when((i > 0) & (i % 2 == 0))
    def _():
        scores(i - 1, 1, False)
        update(i - 2, 0)
        scores(i, 0, True)
        update(i - 1, 1)
        update(i, 0, True)

    @pl.when(i % 2 == 1)
    def _():
        scores(i, 1, True)
        update(i - 1, 0)
        update(i, 1, True)

    @pl.when(i == 0)
    def _():
        update(0, 0, True)

    o_ref[...] = (acc_ref[:, :HEAD_DIM] / acc_ref[:, HEAD_DIM:]).astype(o_ref.dtype)


def _mla_attention(q, k, v, blk=1024):
    H, S, _ = q.shape
    return pl.pallas_call(
        functools.partial(_mla_attn_kernel, blk=blk),
        out_shape=jax.ShapeDtypeStruct((S, H * HEAD_DIM), BF16),
        grid=(H, S // blk),
        in_specs=[
            pl.BlockSpec((None, blk, QK_PAD), lambda h, i: (h, i, 0)),
            pl.BlockSpec((None, S, QK_PAD), lambda h, i: (h, 0, 0)),
            pl.BlockSpec((None, S, 2 * HEAD_DIM), lambda h, i: (h, 0, 0)),
        ],
        out_specs=pl.BlockSpec((blk, HEAD_DIM), lambda h, i: (i, h)),
        scratch_shapes=[
            pltpu.VMEM((2, blk, blk), F32),
            pltpu.VMEM((2, blk, HEAD_DIM), F32),
            pltpu.VMEM((blk, HEAD_DIM), F32),
            pltpu.VMEM((blk, 2 * HEAD_DIM), F32),
        ],
        compiler_params=_params("parallel", "arbitrary"),
        name="mla_attn",
    )(q, k, v)


def _rope_tables(seq):
    inv = 1.0 / (ROPE_THETA ** (jnp.arange(0, ROPE_DIM, 2, dtype=F32) / ROPE_DIM))
    ang = jnp.arange(seq, dtype=F32)[:, None] * inv[None, :]
    cos, sin = jnp.cos(ang), jnp.sin(ang)
    zeros = jnp.zeros((seq, HEAD_DIM - ROPE_DIM), F32)
    cos_pad = jnp.concatenate([cos, cos, zeros], axis=1)
    sin_pad = jnp.concatenate([-sin, sin, zeros], axis=1)
    return cos_pad, sin_pad


def kernel(x, ffn_norm1, ffn1_wg, ffn1_wu, ffn1_wd, mix_norm, ffn_norm2, ffn2_wg, ffn2_wu, ffn2_wd,
           a_wqkv, a_wo, kv_norm, b_wdkv, b_ckv_norm, b_wkr, b_wuk, b_wuv,
           b_wdq, b_cq_norm, b_wuq, b_wo, final_norm):
    B, S, D = x.shape
    assert (B, S, D) == (1, SEQ, D_MODEL)
    bf = lambda w: w.astype(BF16)
    gains = lambda g: g.reshape(g.shape[0], 1, g.shape[-1])

    w1g, w1u, w1d = ffn1_wg, ffn1_wu, ffn1_wd
    w2g, w2u, w2d = ffn2_wg, ffn2_wu, ffn2_wd
    wqkv, wo_a, wo_b = a_wqkv, a_wo, b_wo
    g1, g2, gm = gains(ffn_norm1), gains(ffn_norm2), gains(mix_norm)

    rot = (np.arange(ROPE_DIM) + ROPE_DIM // 2) % ROPE_DIM
    w_kv_down = bf(jnp.concatenate([b_wdkv, b_wkr, b_wkr[:, rot]], axis=1))[None]
    wuk = bf(b_wuk.reshape(KV_LORA, N_HEADS * HEAD_DIM))
    wuv = bf(b_wuv.reshape(KV_LORA, N_HEADS * HEAD_DIM))
    wdq = bf(b_wdq)
    q_rope_w = b_wuq[..., HEAD_DIM:]
    wuq = bf(jnp.concatenate([b_wuq, q_rope_w[..., rot]], axis=-1).reshape(
        b_wuq.shape[0], Q_LORA, N_HEADS * QK_PAD))

    slopes = jnp.asarray(2.0 ** (-8.0 * (np.arange(N_HEADS) + 1) / N_HEADS), dtype=F32)
    cos_pad, sin_pad = _rope_tables(S)
    scale = float((HEAD_DIM + ROPE_DIM) ** -0.5 * np.log2(np.e))

    xs = x[0]
    xs = xs.reshape(S // RESIDUES, RESIDUES, D).transpose(1, 0, 2).reshape(S, D)
    for layer in range(N_A_LAYERS):
        xs = _ffn(xs, g1, w1g, w1u, w1d, layer)
        qkv = _norm_proj(xs, gm, layer, wqkv, layer, tn=256, slab_out=True, tm=2048)
        o = _dilated_attention(qkv, slopes)
        xs = _proj_res(o, wo_a, layer, xs)
        xs = _ffn(xs, g2, w2g, w2u, w2d, layer)
    xs = xs.reshape(RESIDUES, S // RESIDUES, D).transpose(1, 0, 2).reshape(S, D)

    t_kv = _norm_proj(xs, kv_norm.reshape(1, 1, D), 0, w_kv_down, 0, tn=w_kv_down.shape[-1], slab_out=False)
    k_all, v_all = _mla_kv(t_kv, b_ckv_norm.reshape(1, KV_LORA), wuk, wuv, cos_pad, sin_pad)
    gq = gains(b_cq_norm)
    for layer in range(N_A_LAYERS, DEPTH):
        jb = layer - N_A_LAYERS
        xs = _ffn(xs, g1, w1g, w1u, w1d, layer)
        t_q = _norm_proj(xs, gm, layer, wdq, jb, tn=Q_LORA, slab_out=False)
        q_all = _mla_q(t_q, gq, jb, wuq, jb, cos_pad, sin_pad, scale)
        o = _mla_attention(q_all, k_all, v_all)
        xs = _proj_res(o, wo_b, jb, xs)
        last = layer == DEPTH - 1
        xs = _ffn(xs, g2, w2g, w2u, w2d, layer, final_gain=final_norm.reshape(1, D) if last else None)
    return xs[None]
```
